```python
import math
import jax, jax.numpy as jnp
from jax import lax
import numpy as np

D_MODEL = 1024
BATCH = 2
SEQ = 8192
DEPTH = 1

CHUNK = 64
N_META = 16
Q_BLOCK = 128

ATT_HEADS = 8
ATT_HEAD_DIM = 64
ATT_WIDTH = ATT_HEADS * ATT_HEAD_DIM

SSM_WIDTH = 512
SSM_GROUP = 16
SSM_GROUPS = SSM_WIDTH // SSM_GROUP
SSM_STATE = 64

N_BRANCHES = 2
IN_COLS = 3 * ATT_WIDTH + ATT_HEADS + SSM_WIDTH + N_BRANCHES * D_MODEL
SPLIT_IDX = [ATT_WIDTH, 2 * ATT_WIDTH, 3 * ATT_WIDTH, 3 * ATT_WIDTH + ATT_HEADS,
             3 * ATT_WIDTH + ATT_HEADS + SSM_WIDTH]

PEER_HEADS = 8
N_KEYS = 128
N_EXPERTS = N_KEYS * N_KEYS
PEER_TOPK = 16
KEY_DIM = 256
HALF_KEY = KEY_DIM // 2

RMS_EPS = 1e-6
NEG_INF = -1e30

kernel_name = "hybrid_fox_s5_peer_block"


def _rmsnorm(x, g):
    xf = x.astype(jnp.float32)
    y = xf * lax.rsqrt(jnp.mean(xf * xf, axis=-1, keepdims=True) + RMS_EPS)
    return (y * g.astype(jnp.float32)).astype(x.dtype)


def _forgetting_attention(q, k, v, log_f):
    B, L, H, dh = q.shape
    nb = L // Q_BLOCK
    cum = jnp.cumsum(log_f, axis=1)
    cum_k = cum.transpose(0, 2, 1)
    qf = q.astype(jnp.float32) * (dh ** -0.5)
    kf = k.astype(jnp.float32)
    vf = v.astype(jnp.float32)
    q_blocks = qf.reshape(B, nb, Q_BLOCK, H, dh).transpose(1, 0, 2, 3, 4)
    c_blocks = cum.reshape(B, nb, Q_BLOCK, H).transpose(1, 0, 3, 2)
    key_pos = jnp.arange(L)

    def one_block(args):
        qb, cb, blk = args
        s = jnp.einsum('bqhd,bkhd->bhqk', qb, kf)
        s = s + cb[..., :, None] - cum_k[:, :, None, :]
        q_pos = blk * Q_BLOCK + jnp.arange(Q_BLOCK)
        mask = key_pos[None, :] <= q_pos[:, None]
        s = jnp.where(mask, s, NEG_INF)
        p = jax.nn.softmax(s, axis=-1)
        return jnp.einsum('bhqk,bkhd->bqhd', p, vf)

    out = lax.map(one_block, (q_blocks, c_blocks, jnp.arange(nb)))
    return out.transpose(1, 0, 2, 3, 4).reshape(B, L, H * dh)


def _s5_scan(u, a_re, a_im, log_dt, b_re, b_im, c_re, c_im, d_skip):
    B, L, _ = u.shape
    uf = u.astype(jnp.float32).reshape(B, L, SSM_GROUPS, SSM_GROUP)
    ar = a_re.astype(jnp.float32)
    ai = a_im.astype(jnp.float32)
    dt = jnp.exp(log_dt.astype(jnp.float32))[:, None]
    mag = jnp.exp(ar * dt)
    abar_r = mag * jnp.cos(ai * dt)
    abar_i = mag * jnp.sin(ai * dt)
    num_r = abar_r - 1.0
    num_i = abar_i
    den = ar * ar + ai * ai
    coef_r = (num_r * ar + num_i * ai) / den
    coef_i = (num_i * ar - num_r * ai) / den
    br = b_re.astype(jnp.float32)
    bi = b_im.astype(jnp.float32)
    bbar_r = coef_r[..., None] * br - coef_i[..., None] * bi
    bbar_i = coef_r[..., None] * bi + coef_i[..., None] * br
    bu_r = jnp.einsum('blgc,gpc->blgp', uf, bbar_r)
    bu_i = jnp.einsum('blgc,gpc->blgp', uf, bbar_i)
    a_r = jnp.broadcast_to(abar_r, bu_r.shape)
    a_i = jnp.broadcast_to(abar_i, bu_i.shape)

    def combine(e1, e2):
        a1r, a1i, b1r, b1i = e1
        a2r, a2i, b2r, b2i = e2
        return (a2r * a1r - a2i * a1i,
                a2r * a1i + a2i * a1r,
                a2r * b1r - a2i * b1i + b2r,
                a2r * b1i + a2i * b1r + b2i)

    _, _, xs_r, xs_i = lax.associative_scan(combine, (a_r, a_i, bu_r, bu_i), axis=1)
    y = (jnp.einsum('blgp,gcp->blgc', xs_r, c_re.astype(jnp.float32))
         - jnp.einsum('blgp,gcp->blgc', xs_i, c_im.astype(jnp.float32))
         + d_skip.astype(jnp.float32) * uf)
    return y.reshape(B, L, SSM_WIDTH)


def _hybrid_mixer(hn, w_in, b_forget, b_gate, w_att_branch,
                  a_re, a_im, log_dt, b_re, b_im, c_re, c_im, d_skip,
                  w_glu_val, w_glu_gate, w_out):
    B, L, D = hn.shape
    proj = hn @ w_in
    q, k, v, f, u, gate = jnp.split(proj, SPLIT_IDX, axis=-1)
    log_f = jax.nn.log_sigmoid((f + b_forget).astype(jnp.float32))
    att = _forgetting_attention(q.reshape(B, L, ATT_HEADS, ATT_HEAD_DIM),
                                k.reshape(B, L, ATT_HEADS, ATT_HEAD_DIM),
                                v.reshape(B, L, ATT_HEADS, ATT_HEAD_DIM), log_f)
    branch_att = att.astype(hn.dtype) @ w_att_branch
    y = _s5_scan(u, a_re, a_im, log_dt, b_re, b_im, c_re, c_im, d_skip)
    z = jax.nn.gelu(y, approximate=False).astype(hn.dtype)
    branch_ssm = (z @ w_glu_val) * jax.nn.sigmoid(z @ w_glu_gate)
    g = jax.nn.sigmoid((gate + b_gate).astype(jnp.float32)).astype(hn.dtype)
    g = g.reshape(B, L, N_BRANCHES, D)
    merged = g[:, :, 0, :] * branch_att + g[:, :, 1, :] * branch_ssm
    return merged @ w_out


def _peer(hn, w_query, sub_keys, expert_u, expert_v):
    B, L, D = hn.shape
    nb = L // Q_BLOCK
    blocks = hn.reshape(B, nb, Q_BLOCK, D).transpose(1, 0, 2, 3)
    keys = sub_keys.astype(jnp.float32)

    def one_block(xb):
        qry = (xb @ w_query).reshape(B, Q_BLOCK, PEER_HEADS, 2, HALF_KEY).astype(jnp.float32)
        s = jnp.einsum('bqhjd,hjnd->bqhjn', qry, keys)
        v1, i1 = lax.top_k(s[..., 0, :], PEER_TOPK)
        v2, i2 = lax.top_k(s[..., 1, :], PEER_TOPK)
        cand = (v1[..., :, None] + v2[..., None, :]).reshape(B, Q_BLOCK, PEER_HEADS, PEER_TOPK * PEER_TOPK)
        best, pos = lax.top_k(cand, PEER_TOPK)
        e1 = jnp.take_along_axis(i1, pos // PEER_TOPK, axis=-1)
        e2 = jnp.take_along_axis(i2, pos % PEER_TOPK, axis=-1)
        idx = e1 * N_KEYS + e2
        gate = jax.nn.softmax(best, axis=-1)
        u_sel = expert_u[idx]
        act = jax.nn.gelu(jnp.einsum('bqhkd,bqd->bqhk', u_sel, xb).astype(jnp.float32), approximate=False)
        w = (gate * act).astype(xb.dtype)
        return jnp.einsum('bqhk,bqhkd->bqd', w, expert_v[idx])

    out = lax.map(one_block, blocks)
    return out.transpose(1, 0, 2, 3).reshape(B, L, D)


def setup_inputs(seed: int = 0) -> dict:
    key = jax.random.key(seed)
    ks = jax.random.split(key, 24)
    D = D_MODEL
    f32 = jnp.float32

    def nrm(k, shape, scale):
        return jax.random.normal(k, shape, f32) * scale

    x = jax.random.normal(ks[0], (BATCH, SEQ, D), f32)
    meta_tokens = nrm(ks[1], (N_META, D), 1.0)
    mix_norm_g = 1.0 + nrm(ks[2], (DEPTH, D), 0.02)
    w_in = nrm(ks[3], (DEPTH, D, IN_COLS), D ** -0.5)
    b_forget = jax.random.uniform(ks[4], (DEPTH, ATT_HEADS), f32, 1.0, 4.0)
    b_gate = nrm(ks[5], (DEPTH, N_BRANCHES * D), 0.01)
    w_att_branch = nrm(ks[6], (DEPTH, ATT_WIDTH, D), ATT_WIDTH ** -0.5)
    n_idx = jnp.arange(SSM_STATE, dtype=f32)
    ssm_a_re = -0.5 + nrm(ks[7], (DEPTH, SSM_GROUPS, SSM_STATE), 0.01)
    ssm_a_im = math.pi * n_idx + nrm(ks[8], (DEPTH, SSM_GROUPS, SSM_STATE), 0.01)
    ssm_log_dt = jax.random.uniform(ks[9], (DEPTH, SSM_GROUPS), f32, math.log(1e-3), math.log(1e-1))
    ssm_b_re = nrm(ks[10], (DEPTH, SSM_GROUPS, SSM_STATE, SSM_GROUP), (2 * SSM_GROUP) ** -0.5)
    ssm_b_im = nrm(ks[11], (DEPTH, SSM_GROUPS, SSM_STATE, SSM_GROUP), (2 * SSM_GROUP) ** -0.5)
    ssm_c_re = nrm(ks[12], (DEPTH, SSM_GROUPS, SSM_GROUP, SSM_STATE), (2 * SSM_STATE) ** -0.5)
    ssm_c_im = nrm(ks[13], (DEPTH, SSM_GROUPS, SSM_GROUP, SSM_STATE), (2 * SSM_STATE) ** -0.5)
    ssm_d = nrm(ks[14], (DEPTH, SSM_GROUPS, SSM_GROUP), 1.0)
    w_glu_val = nrm(ks[15], (DEPTH, SSM_WIDTH, D), SSM_WIDTH ** -0.5)
    w_glu_gate = nrm(ks[16], (DEPTH, SSM_WIDTH, D), SSM_WIDTH ** -0.5)
    w_out = nrm(ks[17], (DEPTH, D, D), D ** -0.5)
    ffn_norm_g = 1.0 + nrm(ks[18], (DEPTH, D), 0.02)
    w_query = nrm(ks[19], (DEPTH, D, PEER_HEADS * KEY_DIM), D ** -0.5)
    sub_keys = nrm(ks[20], (DEPTH, PEER_HEADS, 2, N_KEYS, HALF_KEY), HALF_KEY ** -0.5)
    expert_u = nrm(ks[21], (DEPTH, N_EXPERTS, D), D ** -0.5)
    expert_v = nrm(ks[22], (DEPTH, N_EXPERTS, D), PEER_HEADS ** -0.5)
    out_norm_g = 1.0 + nrm(ks[23], (D,), 0.02)
    return {"x": x, "meta_tokens": meta_tokens, "mix_norm_g": mix_norm_g, "w_in": w_in,
            "b_forget": b_forget, "b_gate": b_gate, "w_att_branch": w_att_branch,
            "ssm_a_re": ssm_a_re, "ssm_a_im": ssm_a_im, "ssm_log_dt": ssm_log_dt,
            "ssm_b_re": ssm_b_re, "ssm_b_im": ssm_b_im, "ssm_c_re": ssm_c_re, "ssm_c_im": ssm_c_im,
            "ssm_d": ssm_d, "w_glu_val": w_glu_val, "w_glu_gate": w_glu_gate, "w_out": w_out,
            "ffn_norm_g": ffn_norm_g, "w_query": w_query, "sub_keys": sub_keys,
            "expert_u": expert_u, "expert_v": expert_v, "out_norm_g": out_norm_g}


def reference(x, meta_tokens, mix_norm_g, w_in, b_forget, b_gate, w_att_branch,
              ssm_a_re, ssm_a_im, ssm_log_dt, ssm_b_re, ssm_b_im, ssm_c_re, ssm_c_im,
              ssm_d, w_glu_val, w_glu_gate, w_out, ffn_norm_g, w_query, sub_keys,
              expert_u, expert_v, out_norm_g):
    B, S, D = x.shape
    L = S + N_META
    L_pad = ((L + Q_BLOCK - 1) // Q_BLOCK) * Q_BLOCK
    meta = jnp.broadcast_to(meta_tokens.astype(x.dtype)[None], (B, N_META, D))
    pad = jnp.zeros((B, L_pad - L, D), x.dtype)
    h = jnp.concatenate([meta, x, pad], axis=1)
    for layer in range(DEPTH):
        hn = _rmsnorm(h, mix_norm_g[layer])
        h = h + _hybrid_mixer(hn, w_in[layer], b_forget[layer], b_gate[layer], w_att_branch[layer],
                              ssm_a_re[layer], ssm_a_im[layer], ssm_log_dt[layer],
                              ssm_b_re[layer], ssm_b_im[layer], ssm_c_re[layer], ssm_c_im[layer],
                              ssm_d[layer], w_glu_val[layer], w_glu_gate[layer], w_out[layer])
        hn = _rmsnorm(h, ffn_norm_g[layer])
        h = h + _peer(hn, w_query[layer], sub_keys[layer], expert_u[layer], expert_v[layer])
    h = _rmsnorm(h, out_norm_g)
    return h[:, N_META:N_META + S, :]
```

```python
import functools
import math

import jax
import jax.numpy as jnp
import numpy as np
from jax import lax
from jax.experimental import pallas as pl
from jax.experimental.pallas import tpu as pltpu

F32 = jnp.float32
BF16 = jnp.bfloat16

N_META = 16
ATT_HEADS = 8
ATT_HEAD_DIM = 64
ATT_WIDTH = ATT_HEADS * ATT_HEAD_DIM
SSM_WIDTH = 512
SSM_GROUP = 16
SSM_GROUPS = SSM_WIDTH // SSM_GROUP
SSM_STATE = 64
SSM_LANES = SSM_GROUPS * SSM_STATE
PEER_HEADS = 8
N_KEYS = 128
PEER_TOPK = 16
HALF_KEY = 128
RMS_EPS = 1e-6
NEG_INF = -1e30

LANE = 128
TOKEN_TILE = 640
SSM_CHUNK = 128
EXPERT_CHUNK = 1024
VMEM_LIMIT = 56 * 1024 * 1024


def _rms(x, g):
    return x * lax.rsqrt(jnp.mean(x * x, axis=-1, keepdims=True) + RMS_EPS) * g


def _gelu(x):
    return 0.5 * x * (1.0 + lax.erf(x * (2.0 ** -0.5)))


def _sigmoid(x):
    return 1.0 / (1.0 + jnp.exp(-x))


def _log_sigmoid(x):
    return jnp.minimum(x, 0.0) - jnp.log1p(jnp.exp(-jnp.abs(x)))


def _split3(c):
    hi = c.astype(BF16).astype(F32)
    r = c - hi
    mid = r.astype(BF16).astype(F32)
    lo = (r - mid).astype(BF16).astype(F32)
    return hi, mid, lo


def _inproj_kernel(h_ref, g_ref, w_ref, bf_ref, q_ref, k_ref, v_ref, u_ref, carry_ref):
    tm = h_ref.shape[0]

    @pl.when(pl.program_id(1) == 0)
    def _():
        carry_ref[...] = jnp.zeros_like(carry_ref)

    hn = _rms(h_ref[...], g_ref[...]).astype(BF16)
    proj = jnp.dot(hn, w_ref[...], preferred_element_type=F32)
    u_ref[...] = proj[:, 3 * ATT_WIDTH:3 * ATT_WIDTH + SSM_WIDTH]

    log_f = _log_sigmoid(proj[:, 3 * ATT_WIDTH + SSM_WIDTH:] + bf_ref[...])
    row = lax.broadcasted_iota(jnp.int32, (tm, tm), 0)
    col = lax.broadcasted_iota(jnp.int32, (tm, tm), 1)
    tri = (row >= col).astype(F32)
    c = jnp.dot(tri, log_f, preferred_element_type=F32, precision=lax.Precision.HIGHEST) + carry_ref[...]
    carry_ref[...] = c[tm - 1:tm, :]
    hi, mid, lo = _split3(c)

    lane = lax.broadcasted_iota(jnp.int32, (tm, ATT_HEAD_DIM), 1)
    ones3 = jnp.where(lane < 3, 1.0, 0.0)
    v_ext = jnp.where(lane == 0, 1.0, 0.0).astype(BF16)
    for h in range(ATT_HEADS):
        ch, cm, cl = hi[:, h:h + 1], mid[:, h:h + 1], lo[:, h:h + 1]
        cvec = jnp.where(lane == 0, ch, jnp.where(lane == 1, cm, jnp.where(lane == 2, cl, 0.0)))
        q_ext = cvec + jnp.where((lane >= 3) & (lane < 6), 1.0, 0.0)
        k_ext = ones3 - jnp.where(lane == 3, ch, jnp.where(lane == 4, cm, jnp.where(lane == 5, cl, 0.0)))
        sl = slice(h * ATT_HEAD_DIM, (h + 1) * ATT_HEAD_DIM)
        q_ref[h] = jnp.concatenate([proj[:, sl].astype(BF16), q_ext.astype(BF16)], axis=-1)
        k_ref[h] = jnp.concatenate(
            [proj[:, ATT_WIDTH + h * ATT_HEAD_DIM:ATT_WIDTH + (h + 1) * ATT_HEAD_DIM].astype(BF16),
             k_ext.astype(BF16)], axis=-1)
        v_ref[h] = jnp.concatenate(
            [proj[:, 2 * ATT_WIDTH + h * ATT_HEAD_DIM:2 * ATT_WIDTH + (h + 1) * ATT_HEAD_DIM].astype(BF16),
             v_ext], axis=-1)


def _inproj(h, g, w, bf, tm):
    B, L, D = h.shape
    ncol = w.shape[1]
    head_shape = jax.ShapeDtypeStruct((B, ATT_HEADS, L, LANE), BF16)
    head_spec = pl.BlockSpec((None, ATT_HEADS, tm, LANE), lambda b, i: (b, 0, i, 0))
    return pl.pallas_call(
        _inproj_kernel,
        grid=(B, L // tm),
        in_specs=[
            pl.BlockSpec((None, tm, D), lambda b, i: (b, i, 0)),
            pl.BlockSpec((1, D), lambda b, i: (0, 0)),
            pl.BlockSpec((D, ncol), lambda b, i: (0, 0)),
            pl.BlockSpec((1, LANE), lambda b, i: (0, 0)),
        ],
        out_specs=[head_spec, head_spec, head_spec,
                   pl.BlockSpec((None, tm, SSM_WIDTH), lambda b, i: (b, i, 0))],
        out_shape=[head_shape, head_shape, head_shape,
                   jax.ShapeDtypeStruct((B, L, SSM_WIDTH), F32)],
        scratch_shapes=[pltpu.VMEM((1, LANE), F32)],
        compiler_params=pltpu.CompilerParams(
            dimension_semantics=("arbitrary", "arbitrary"), vmem_limit_bytes=VMEM_LIMIT),
        name="inproj",
    )(h, g, w, bf)


def _attn_kernel(qi_tab, ki_tab, q_ref, k_ref, v_ref, o_ref, m_ref, acc_ref):
    step = pl.program_id(2)
    qi = qi_tab[step]
    ki = ki_tab[step]
    tq, tk = q_ref.shape[0], k_ref.shape[0]

    @pl.when(ki == 0)
    def _():
        m_ref[...] = jnp.full_like(m_ref, NEG_INF)
        acc_ref[...] = jnp.zeros_like(acc_ref)

    def scores():
        return lax.dot_general(q_ref[...], k_ref[...], (((1,), (1,)), ((), ())),
                               preferred_element_type=F32)

    def update(s):
        m_prev = m_ref[...]
        m_new = jnp.maximum(m_prev, jnp.max(s, axis=1, keepdims=True))
        alpha = jnp.exp(m_prev - m_new)
        p = jnp.exp(s - m_new[:, :1]).astype(BF16)
        acc_ref[...] = alpha * acc_ref[...] + jnp.dot(p, v_ref[...], preferred_element_type=F32)
        m_ref[...] = m_new

    @pl.when(ki < qi)
    def _():
        update(scores())

    @pl.when(ki == qi)
    def _():
        row = lax.broadcasted_iota(jnp.int32, (tq, tk), 0)
        col = lax.broadcasted_iota(jnp.int32, (tq, tk), 1)
        update(jnp.where(col <= row, scores(), NEG_INF))
        acc = acc_ref[...]
        o_ref[...] = (acc[:, :ATT_HEAD_DIM] / acc[:, ATT_HEAD_DIM:ATT_HEAD_DIM + 1]).astype(o_ref.dtype)


def _attn(q, k, v, tq):
    B, H, L, _ = q.shape
    nq = L // tq
    pairs = [(a, b) for a in range(nq) for b in range(a + 1)]
    qi_tab = jnp.asarray(np.array([p[0] for p in pairs], np.int32))
    ki_tab = jnp.asarray(np.array([p[1] for p in pairs], np.int32))
    q_spec = pl.BlockSpec((None, None, tq, LANE), lambda b, h, s, qt, kt: (b, h, qt[s], 0))
    kv_spec = pl.BlockSpec((None, None, tq, LANE), lambda b, h, s, qt, kt: (b, h, kt[s], 0))
    return pl.pallas_call(
        _attn_kernel,
        grid_spec=pltpu.PrefetchScalarGridSpec(
            num_scalar_prefetch=2,
            grid=(B, H, len(pairs)),
            in_specs=[q_spec, kv_spec, kv_spec],
            out_specs=pl.BlockSpec((None, None, tq, ATT_HEAD_DIM),
                                   lambda b, h, s, qt, kt: (b, h, qt[s], 0)),
            scratch_shapes=[pltpu.VMEM((tq, LANE), F32), pltpu.VMEM((tq, LANE), F32)],
        ),
        out_shape=jax.ShapeDtypeStruct((B, H, L, ATT_HEAD_DIM), BF16),
        compiler_params=pltpu.CompilerParams(
            dimension_semantics=("arbitrary", "arbitrary", "arbitrary"), vmem_limit_bytes=VMEM_LIMIT),
        name="attn",
    )(qi_tab, ki_tab, q, k, v)


def _ssm_kernel(u_ref, bre_ref, bim_ref, cre_ref, cim_ref, d_ref,
                negr_ref, negi_ref, posr_ref, posi_ref, nxtr_ref, nxti_ref, z_ref, carry_ref):
    tc = u_ref.shape[0]

    @pl.when(pl.program_id(1) == 0)
    def _():
        carry_ref[...] = jnp.zeros_like(carry_ref)

    u = u_ref[...]
    ub = u.astype(BF16)
    bur = jnp.dot(ub, bre_ref[...], preferred_element_type=F32)
    bui = jnp.dot(ub, bim_ref[...], preferred_element_type=F32)
    nr, ni = negr_ref[...], negi_ref[...]
    zr = nr * bur - ni * bui
    zi = nr * bui + ni * bur
    row = lax.broadcasted_iota(jnp.int32, (tc, tc), 0)
    col = lax.broadcasted_iota(jnp.int32, (tc, tc), 1)
    tri = (row >= col).astype(F32)
    sr = jnp.dot(tri, zr, preferred_element_type=F32, precision=lax.Precision.HIGHEST)
    si = jnp.dot(tri, zi, preferred_element_type=F32, precision=lax.Precision.HIGHEST)
    cr, ci = carry_ref[0:1, :], carry_ref[1:2, :]
    pr, pi = posr_ref[...], posi_ref[...]
    qr, qi = nxtr_ref[...], nxti_ref[...]
    xr = pr * sr - pi * si + (qr * cr - qi * ci)
    xi = pr * si + pi * sr + (qr * ci + qi * cr)
    carry_ref[0:1, :] = xr[tc - 1:tc, :]
    carry_ref[1:2, :] = xi[tc - 1:tc, :]
    y = (jnp.dot(xr.astype(BF16), cre_ref[...], preferred_element_type=F32)
         - jnp.dot(xi.astype(BF16), cim_ref[...], preferred_element_type=F32)
         + d_ref[...] * u)
    z_ref[...] = _gelu(y).astype(z_ref.dtype)


def _ssm(u, bre, bim, cre, cim, d, tabs):
    B, L, W = u.shape
    tc = SSM_CHUNK
    const = lambda shape: pl.BlockSpec(shape, lambda b, i: (0, 0))
    return pl.pallas_call(
        _ssm_kernel,
        grid=(B, L // tc),
        in_specs=[pl.BlockSpec((None, tc, W), lambda b, i: (b, i, 0)),
                  const((W, SSM_LANES)), const((W, SSM_LANES)),
                  const((SSM_LANES, W)), const((SSM_LANES, W)), const((1, W))]
                 + [const((tc, SSM_LANES))] * 6,
        out_specs=pl.BlockSpec((None, tc, W), lambda b, i: (b, i, 0)),
        out_shape=jax.ShapeDtypeStruct((B, L, W), BF16),
        scratch_shapes=[pltpu.VMEM((2, SSM_LANES), F32)],
        compiler_params=pltpu.CompilerParams(
            dimension_semantics=("arbitrary", "arbitrary"), vmem_limit_bytes=VMEM_LIMIT),
        name="ssm",
    )(u, bre, bim, cre, cim, d, *tabs)


def _ssm_params(a_re, a_im, log_dt, b_re, b_im, c_re, c_im, d_skip):
    G, P = a_re.shape
    dt = jnp.exp(log_dt.astype(F32))[:, None]
    ar, ai = a_re.astype(F32), a_im.astype(F32)
    mag = jnp.exp(ar * dt)
    abar_r = mag * jnp.cos(ai * dt)
    abar_i = mag * jnp.sin(ai * dt)
    num_r, num_i = abar_r - 1.0, abar_i
    den = ar * ar + ai * ai
    coef_r = (num_r * ar + num_i * ai) / den
    coef_i = (num_i * ar - num_r * ai) / den
    br, bi = b_re.astype(F32), b_im.astype(F32)
    bbar_r = coef_r[..., None] * br - coef_i[..., None] * bi
    bbar_i = coef_r[..., None] * bi + coef_i[..., None] * br
    eye = jnp.eye(G, dtype=F32)
    bd_in = lambda w: jnp.einsum('gpc,gh->gchp', w, eye).reshape(G * SSM_GROUP, G * P)
    bd_out = lambda w: jnp.einsum('gcp,gh->gphc', w.astype(F32), eye).reshape(G * P, G * SSM_GROUP)
    k = jnp.arange(SSM_CHUNK, dtype=F32)[:, None]
    lr = (ar * dt).reshape(1, G * P)
    li = (ai * dt).reshape(1, G * P)

    def power(kk):
        m = jnp.exp(kk * lr)
        return m * jnp.cos(kk * li), m * jnp.sin(kk * li)

    neg, pos, nxt = power(-k), power(k), power(k + 1.0)
    return (bd_in(bbar_r).astype(BF16), bd_in(bbar_i).astype(BF16),
            bd_out(c_re).astype(BF16), bd_out(c_im).astype(BF16),
            d_skip.astype(F32).reshape(1, G * SSM_GROUP), (*neg, *pos, *nxt))


def _merge_kernel(h_ref, att_ref, z_ref, g1_ref, wg_ref, bg_ref, wa_ref, wv_ref, wz_ref, wo_ref, g2_ref,
                  h1_ref, hn2_ref):
    D = h_ref.shape[1]
    h = h_ref[...]
    hn = _rms(h, g1_ref[...]).astype(BF16)
    gate = _sigmoid(jnp.dot(hn, wg_ref[...], preferred_element_type=F32) + bg_ref[...])
    att = jnp.concatenate([att_ref[i] for i in range(ATT_HEADS)], axis=-1)
    branch_att = jnp.dot(att, wa_ref[...], preferred_element_type=F32)
    z = z_ref[...]
    branch_ssm = (jnp.dot(z, wv_ref[...], preferred_element_type=F32)
                  * _sigmoid(jnp.dot(z, wz_ref[...], preferred_element_type=F32)))
    merged = gate[:, :D] * branch_att + gate[:, D:] * branch_ssm
    h1 = h + jnp.dot(merged.astype(BF16), wo_ref[...], preferred_element_type=F32)
    h1_ref[...] = h1
    hn2_ref[...] = _rms(h1, g2_ref[...]).astype(hn2_ref.dtype)


def _merge(h, att, z, g1, wg, bg, wa, wv, wz, wo, g2, tm):
    B, L, D = h.shape
    const = lambda shape: pl.BlockSpec(shape, lambda b, i: (0,) * len(shape))
    tok = lambda w: pl.BlockSpec((None, tm, w), lambda b, i: (b, i, 0))
    return pl.pallas_call(
        _merge_kernel,
        grid=(B, L // tm),
        in_specs=[tok(D),
                  pl.BlockSpec((None, ATT_HEADS, tm, ATT_HEAD_DIM), lambda b, i: (b, 0, i, 0)),
                  tok(SSM_WIDTH), const((1, D)), const((D, 2 * D)), const((1, 2 * D)),
                  const((ATT_WIDTH, D)), const((SSM_WIDTH, D)), const((SSM_WIDTH, D)),
                  const((D, D)), const((1, D))],
        out_specs=[tok(D), tok(D)],
        out_shape=[jax.ShapeDtypeStruct((B, L, D), F32), jax.ShapeDtypeStruct((B, L, D), BF16)],
        compiler_params=pltpu.CompilerParams(
            dimension_semantics=("arbitrary", "arbitrary"), vmem_limit_bytes=VMEM_LIMIT),
        name="merge",
    )(h, att, z, g1, wg, bg, wa, wv, wz, wo, g2)


def _top16(s):
    vals = []
    for _ in range(PEER_TOPK):
        m = jnp.max(s, axis=0, keepdims=True)
        vals.append(m)
        s = jnp.where(s == m, -jnp.inf, s)
    return vals


def _peer_kernel(hn_ref, h1_ref, wq_ref, keys_ref, u_ref, vt_ref, go_ref, o_ref,
                 s1_ref, s2_ref, e2_ref, m1_ref, tau_ref, act_ref, w_ref, acc_ref):
    c = pl.program_id(2)
    tq = hn_ref.shape[0]
    first_keys = EXPERT_CHUNK // N_KEYS

    @pl.when(c == 0)
    def _():
        acc_ref[...] = jnp.zeros_like(acc_ref)
        qt = lax.dot_general(wq_ref[...], hn_ref[...], (((1,), (1,)), ((), ())),
                             preferred_element_type=F32).astype(BF16)
        for h in range(PEER_HEADS):
            tops = []
            for j in range(2):
                r = (2 * h + j) * HALF_KEY
                s = jnp.dot(keys_ref[2 * h + j], qt[r:r + HALF_KEY, :], preferred_element_type=F32)
                top = _top16(s)
                masked = jnp.where(s >= top[-1], s, -jnp.inf)
                (s1_ref if j == 0 else s2_ref)[h] = masked
                tops.append(top)
            v1, v2 = tops
            cand = jnp.concatenate(
                [v1[0] + jnp.concatenate(v2, axis=0)]
                + [v1[a] + jnp.concatenate(v2[:8], axis=0) for a in range(1, PEER_TOPK)], axis=0)
            best = _top16(cand)
            zsum = sum(jnp.exp(b - best[0]) for b in best)
            m1_ref[h] = v1[0]
            tau_ref[h] = best[-1]
            e2_ref[h] = jnp.exp(s2_ref[h] - v2[0]) / zsum

    act_ref[...] = lax.dot_general(u_ref[...], hn_ref[...], (((1,), (1,)), ((), ())),
                                   preferred_element_type=F32)

    def per_first_key(ii, carry):
        i = c * first_keys + ii
        g = jnp.zeros((N_KEYS, tq), F32)
        for h in range(PEER_HEADS):
            r1 = s1_ref[h, pl.ds(i, 1), :]
            e1 = jnp.exp(r1 - m1_ref[h])
            pair = r1 + s2_ref[h]
            g = g + jnp.where(pair >= tau_ref[h], e1 * e2_ref[h], 0.0)
        off = pl.multiple_of(ii * N_KEYS, N_KEYS)
        w_ref[pl.ds(off, N_KEYS), :] = (g * _gelu(act_ref[pl.ds(off, N_KEYS), :])).astype(BF16)
        return carry

    lax.fori_loop(0, first_keys, per_first_key, 0)
    acc_ref[...] += jnp.dot(vt_ref[...], w_ref[...], preferred_element_type=F32)

    @pl.when(c == pl.num_programs(2) - 1)
    def _():
        h2 = h1_ref[...] + acc_ref[...].T
        o_ref[...] = _rms(h2, go_ref[...])


def _peer(hn2, h1, wq_t, keys, u, v_t, g_out, tq):
    B, L, D = h1.shape
    n_experts = u.shape[0]
    tok = pl.BlockSpec((None, tq, D), lambda b, i, c: (b, i, 0))
    const = lambda shape: pl.BlockSpec(shape, lambda b, i, c: (0,) * len(shape))
    per_head = lambda rows: pltpu.VMEM((PEER_HEADS, rows, tq), F32)
    return pl.pallas_call(
        _peer_kernel,
        grid=(B, L // tq, n_experts // EXPERT_CHUNK),
        in_specs=[tok, tok, const(wq_t.shape), const(keys.shape),
                  pl.BlockSpec((EXPERT_CHUNK, D), lambda b, i, c: (c, 0)),
                  pl.BlockSpec((D, EXPERT_CHUNK), lambda b, i, c: (0, c)),
                  const((1, D))],
        out_specs=tok,
        out_shape=jax.ShapeDtypeStruct((B, L, D), F32),
        scratch_shapes=[per_head(N_KEYS), per_head(N_KEYS), per_head(N_KEYS), per_head(1), per_head(1),
                        pltpu.VMEM((EXPERT_CHUNK, tq), F32), pltpu.VMEM((EXPERT_CHUNK, tq), BF16),
                        pltpu.VMEM((D, tq), F32)],
        compiler_params=pltpu.CompilerParams(
            dimension_semantics=("arbitrary", "arbitrary", "arbitrary"), vmem_limit_bytes=VMEM_LIMIT),
        name="peer",
    )(hn2, h1, wq_t, keys, u, v_t, g_out)


def kernel(x, meta_tokens, mix_norm_g, w_in, b_forget, b_gate, w_att_branch, ssm_a_re, ssm_a_im, ssm_log_dt,
           ssm_b_re, ssm_b_im, ssm_c_re, ssm_c_im, ssm_d, w_glu_val, w_glu_gate, w_out, ffn_norm_g,
           w_query, sub_keys, expert_u, expert_v, out_norm_g):
    B, S, D = x.shape
    tm = TOKEN_TILE
    L = S + N_META
    L_pad = -(-L // tm) * tm
    meta = jnp.broadcast_to(meta_tokens.astype(x.dtype)[None], (B, N_META, D))
    h = jnp.concatenate([meta, x, jnp.zeros((B, L_pad - L, D), x.dtype)], axis=1)
    row = lambda v: v.astype(F32).reshape(1, -1)

    for layer in range(w_in.shape[0]):
        wl = w_in[layer]
        o_f = 3 * ATT_WIDTH
        o_u = o_f + ATT_HEADS
        o_g = o_u + SSM_WIDTH
        w1 = jnp.concatenate(
            [wl[:, :ATT_WIDTH] * (ATT_HEAD_DIM ** -0.5), wl[:, ATT_WIDTH:o_f], wl[:, o_u:o_g],
             jnp.pad(wl[:, o_f:o_u], ((0, 0), (0, LANE - ATT_HEADS)))], axis=1).astype(BF16)
        bf = jnp.pad(b_forget[layer].astype(F32), (0, LANE - ATT_HEADS)).reshape(1, LANE)
        q, k, v, u = _inproj(h, row(mix_norm_g[layer]), w1, bf, tm)
        att = _attn(q, k, v, tm)
        bre, bim, cre, cim, dsk, tabs = _ssm_params(
            ssm_a_re[layer], ssm_a_im[layer], ssm_log_dt[layer], ssm_b_re[layer], ssm_b_im[layer],
            ssm_c_re[layer], ssm_c_im[layer], ssm_d[layer])
        z = _ssm(u, bre, bim, cre, cim, dsk, tabs)
        h1, hn2 = _merge(h, att, z, row(mix_norm_g[layer]), wl[:, o_g:].astype(BF16), row(b_gate[layer]),
                         w_att_branch[layer].astype(BF16), w_glu_val[layer].astype(BF16),
                         w_glu_gate[layer].astype(BF16), w_out[layer].astype(BF16), row(ffn_norm_g[layer]), tm)
        assert layer == w_in.shape[0] - 1, "only the last layer's PEER kernel fuses the output norm"
        keys = sub_keys[layer].astype(BF16).reshape(2 * PEER_HEADS, N_KEYS, HALF_KEY)
        h = _peer(hn2, h1, w_query[layer].T.astype(BF16), keys, expert_u[layer].astype(BF16),
                  expert_v[layer].T.astype(BF16), row(out_norm_g), tm)
    return h[:, N_META:N_META + S, :]
```

```python
import functools
import math

import jax
import jax.numpy as jnp
import numpy as np
from jax import lax
from jax.experimental import pallas as pl
from jax.experimental.pallas import tpu as pltpu

F32 = jnp.float32
BF16 = jnp.bfloat16

N_META = 16
ATT_HEADS = 8
ATT_HEAD_DIM = 64
ATT_WIDTH = ATT_HEADS * ATT_HEAD_DIM
SSM_WIDTH = 512
SSM_GROUP = 16
SSM_GROUPS = SSM_WIDTH // SSM_GROUP
SSM_STATE = 64
SSM_LANES = SSM_GROUPS * SSM_STATE
PEER_HEADS = 8
N_KEYS = 128
PEER_TOPK = 16
HALF_KEY = 128
RMS_EPS = 1e-6
NEG_INF = -1e30

LANE = 128
TOKEN_TILE = 640
SSM_CHUNK = 128
EXPERT_CHUNK = 1024
VMEM_LIMIT = 56 * 1024 * 1024


def _rms(x, g):
    return x * lax.rsqrt(jnp.mean(x * x, axis=-1, keepdims=True) + RMS_EPS) * g


def _gelu(x):
    return 0.5 * x * (1.0 + lax.erf(x * (2.0 ** -0.5)))


def _sigmoid(x):
    return 1.0 / (1.0 + jnp.exp(-x))


def _log_sigmoid(x):
    return jnp.minimum(x, 0.0) - jnp.log1p(jnp.exp(-jnp.abs(x)))


def _split3(c):
    hi = c.astype(BF16).astype(F32)
    r = c - hi
    mid = r.astype(BF16).astype(F32)
    lo = (r - mid).astype(BF16).astype(F32)
    return hi, mid, lo


def _inproj_kernel(h_ref, g_ref, w_ref, bf_ref, q_ref, k_ref, v_ref, u_ref, carry_ref):
    tm = h_ref.shape[0]

    @pl.when(pl.program_id(1) == 0)
    def _():
        carry_ref[...] = jnp.zeros_like(carry_ref)

    hn = _rms(h_ref[...], g_ref[...]).astype(BF16)
    proj = jnp.dot(hn, w_ref[...], preferred_element_type=F32)
    u_ref[...] = proj[:, 3 * ATT_WIDTH:3 * ATT_WIDTH + SSM_WIDTH]

    log_f = _log_sigmoid(proj[:, 3 * ATT_WIDTH + SSM_WIDTH:] + bf_ref[...])
    row = lax.broadcasted_iota(jnp.int32, (tm, tm), 0)
    col = lax.broadcasted_iota(jnp.int32, (tm, tm), 1)
    tri = (row >= col).astype(F32)
    c = jnp.dot(tri, log_f, preferred_element_type=F32, precision=lax.Precision.HIGHEST) + carry_ref[...]
    carry_ref[...] = c[tm - 1:tm, :]
    hi, mid, lo = _split3(c)

    lane = lax.broadcasted_iota(jnp.int32, (tm, ATT_HEAD_DIM), 1)
    ones3 = jnp.where(lane < 3, 1.0, 0.0)
    v_ext = jnp.where(lane == 0, 1.0, 0.0).astype(BF16)
    for h in range(ATT_HEADS):
        ch, cm, cl = hi[:, h:h + 1], mid[:, h:h + 1], lo[:, h:h + 1]
        cvec = jnp.where(lane == 0, ch, jnp.where(lane == 1, cm, jnp.where(lane == 2, cl, 0.0)))
        q_ext = cvec + jnp.where((lane >= 3) & (lane < 6), 1.0, 0.0)
        k_ext = ones3 - jnp.where(lane == 3, ch, jnp.where(lane == 4, cm, jnp.where(lane == 5, cl, 0.0)))
        sl = slice(h * ATT_HEAD_DIM, (h + 1) * ATT_HEAD_DIM)
        q_ref[h] = jnp.concatenate([proj[:, sl].astype(BF16), q_ext.astype(BF16)], axis=-1)
        k_ref[h] = jnp.concatenate(
            [proj[:, ATT_WIDTH + h * ATT_HEAD_DIM:ATT_WIDTH + (h + 1) * ATT_HEAD_DIM].astype(BF16),
             k_ext.astype(BF16)], axis=-1)
        v_ref[h] = jnp.concatenate(
            [proj[:, 2 * ATT_WIDTH + h * ATT_HEAD_DIM:2 * ATT_WIDTH + (h + 1) * ATT_HEAD_DIM].astype(BF16),
             v_ext], axis=-1)


def _inproj(h, g, w, bf, tm):
    B, L, D = h.shape
    ncol = w.shape[1]
    head_shape = jax.ShapeDtypeStruct((B, ATT_HEADS, L, LANE), BF16)
    head_spec = pl.BlockSpec((None, ATT_HEADS, tm, LANE), lambda b, i: (b, 0, i, 0))
    return pl.pallas_call(
        _inproj_kernel,
        grid=(B, L // tm),
        in_specs=[
            pl.BlockSpec((None, tm, D), lambda b, i: (b, i, 0)),
            pl.BlockSpec((1, D), lambda b, i: (0, 0)),
            pl.BlockSpec((D, ncol), lambda b, i: (0, 0)),
            pl.BlockSpec((1, LANE), lambda b, i: (0, 0)),
        ],
        out_specs=[head_spec, head_spec, head_spec,
                   pl.BlockSpec((None, tm, SSM_WIDTH), lambda b, i: (b, i, 0))],
        out_shape=[head_shape, head_shape, head_shape,
                   jax.ShapeDtypeStruct((B, L, SSM_WIDTH), F32)],
        scratch_shapes=[pltpu.VMEM((1, LANE), F32)],
        compiler_params=pltpu.CompilerParams(
            dimension_semantics=("arbitrary", "arbitrary"), vmem_limit_bytes=VMEM_LIMIT),
        name="inproj",
    )(h, g, w, bf)


def _attn_kernel(qi_tab, ki_tab, q_ref, k_ref, v_ref, o_ref, m_ref, acc_ref):
    step = pl.program_id(2)
    qi = qi_tab[step]
    ki = ki_tab[step]
    tq, tk = q_ref.shape[0], k_ref.shape[0]

    @pl.when(ki == 0)
    def _():
        m_ref[...] = jnp.full_like(m_ref, NEG_INF)
        acc_ref[...] = jnp.zeros_like(acc_ref)

    def scores():
        return lax.dot_general(q_ref[...], k_ref[...], (((1,), (1,)), ((), ())),
                               preferred_element_type=F32)

    def update(s):
        m_prev = m_ref[...]
        m_new = jnp.maximum(m_prev, jnp.max(s, axis=1, keepdims=True))
        alpha = jnp.exp(m_prev - m_new)
        p = jnp.exp(s - m_new[:, :1]).astype(BF16)
        acc_ref[...] = alpha * acc_ref[...] + jnp.dot(p, v_ref[...], preferred_element_type=F32)
        m_ref[...] = m_new

    @pl.when(ki < qi)
    def _():
        update(scores())

    @pl.when(ki == qi)
    def _():
        row = lax.broadcasted_iota(jnp.int32, (tq, tk), 0)
        col = lax.broadcasted_iota(jnp.int32, (tq, tk), 1)
        update(jnp.where(col <= row, scores(), NEG_INF))
        acc = acc_ref[...]
        o_ref[...] = (acc[:, :ATT_HEAD_DIM] / acc[:, ATT_HEAD_DIM:ATT_HEAD_DIM + 1]).astype(o_ref.dtype)


def _attn(q, k, v, tq):
    B, H, L, _ = q.shape
    nq = L // tq
    pairs = [(a, b) for a in range(nq) for b in range(a + 1)]
    qi_tab = jnp.asarray(np.array([p[0] for p in pairs], np.int32))
    ki_tab = jnp.asarray(np.array([p[1] for p in pairs], np.int32))
    q_spec = pl.BlockSpec((None, None, tq, LANE), lambda b, h, s, qt, kt: (b, h, qt[s], 0))
    kv_spec = pl.BlockSpec((None, None, tq, LANE), lambda b, h, s, qt, kt: (b, h, kt[s], 0))
    return pl.pallas_call(
        _attn_kernel,
        grid_spec=pltpu.PrefetchScalarGridSpec(
            num_scalar_prefetch=2,
            grid=(B, H, len(pairs)),
            in_specs=[q_spec, kv_spec, kv_spec],
            out_specs=pl.BlockSpec((None, None, tq, ATT_HEAD_DIM),
                                   lambda b, h, s, qt, kt: (b, h, qt[s], 0)),
            scratch_shapes=[pltpu.VMEM((tq, LANE), F32), pltpu.VMEM((tq, LANE), F32)],
        ),
        out_shape=jax.ShapeDtypeStruct((B, H, L, ATT_HEAD_DIM), BF16),
        compiler_params=pltpu.CompilerParams(
            dimension_semantics=("arbitrary", "arbitrary", "arbitrary"), vmem_limit_bytes=VMEM_LIMIT),
        name="attn",
    )(qi_tab, ki_tab, q, k, v)


def _ssm_kernel(u_ref, bre_ref, bim_ref, cre_ref, cim_ref, d_ref,
                negr_ref, negi_ref, posr_ref, posi_ref, nxtr_ref, nxti_ref, z_ref, carry_ref):
    tc = u_ref.shape[0]

    @pl.when(pl.program_id(1) == 0)
    def _():
        carry_ref[...] = jnp.zeros_like(carry_ref)

    u = u_ref[...]
    ub = u.astype(BF16)
    bur = jnp.dot(ub, bre_ref[...], preferred_element_type=F32)
    bui = jnp.dot(ub, bim_ref[...], preferred_element_type=F32)
    nr, ni = negr_ref[...], negi_ref[...]
    zr = nr * bur - ni * bui
    zi = nr * bui + ni * bur
    row = lax.broadcasted_iota(jnp.int32, (tc, tc), 0)
    col = lax.broadcasted_iota(jnp.int32, (tc, tc), 1)
    tri = (row >= col).astype(F32)
    sr = jnp.dot(tri, zr, preferred_element_type=F32, precision=lax.Precision.HIGHEST)
    si = jnp.dot(tri, zi, preferred_element_type=F32, precision=lax.Precision.HIGHEST)
    cr, ci = carry_ref[0:1, :], carry_ref[1:2, :]
    pr, pi = posr_ref[...], posi_ref[...]
    qr, qi = nxtr_ref[...], nxti_ref[...]
    xr = pr * sr - pi * si + (qr * cr - qi * ci)
    xi = pr * si + pi * sr + (qr * ci + qi * cr)
    carry_ref[0:1, :] = xr[tc - 1:tc, :]
    carry_ref[1:2, :] = xi[tc - 1:tc, :]
    y = (jnp.dot(xr.astype(BF16), cre_ref[...], preferred_element_type=F32)
         - jnp.dot(xi.astype(BF16), cim_ref[...], preferred_element_type=F32)
         + d_ref[...] * u)
    z_ref[...] = _gelu(y).astype(z_ref.dtype)


def _ssm(u, bre, bim, cre, cim, d, tabs):
    B, L, W = u.shape
    tc = SSM_CHUNK
    const = lambda shape: pl.BlockSpec(shape, lambda b, i: (0, 0))
    return pl.pallas_call(
        _ssm_kernel,
        grid=(B, L // tc),
        in_specs=[pl.BlockSpec((None, tc, W), lambda b, i: (b, i, 0)),
                  const((W, SSM_LANES)), const((W, SSM_LANES)),
                  const((SSM_LANES, W)), const((SSM_LANES, W)), const((1, W))]
                 + [const((tc, SSM_LANES))] * 6,
        out_specs=pl.BlockSpec((None, tc, W), lambda b, i: (b, i, 0)),
        out_shape=jax.ShapeDtypeStruct((B, L, W), BF16),
        scratch_shapes=[pltpu.VMEM((2, SSM_LANES), F32)],
        compiler_params=pltpu.CompilerParams(
            dimension_semantics=("arbitrary", "arbitrary"), vmem_limit_bytes=VMEM_LIMIT),
        name="ssm",
    )(u, bre, bim, cre, cim, d, *tabs)


def _ssm_params(a_re, a_im, log_dt, b_re, b_im, c_re, c_im, d_skip):
    G, P = a_re.shape
    dt = jnp.exp(log_dt.astype(F32))[:, None]
    ar, ai = a_re.astype(F32), a_im.astype(F32)
    mag = jnp.exp(ar * dt)
    abar_r = mag * jnp.cos(ai * dt)
    abar_i = mag * jnp.sin(ai * dt)
    num_r, num_i = abar_r - 1.0, abar_i
    den = ar * ar + ai * ai
    coef_r = (num_r * ar + num_i * ai) / den
    coef_i = (num_i * ar - num_r * ai) / den
    br, bi = b_re.astype(F32), b_im.astype(F32)
    bbar_r = coef_r[..., None] * br - coef_i[..., None] * bi
    bbar_i = coef_r[..., None] * bi + coef_i[..., None] * br
    eye = jnp.eye(G, dtype=F32)
    bd_in = lambda w: jnp.einsum('gpc,gh->gchp', w, eye).reshape(G * SSM_GROUP, G * P)
    bd_out = lambda w: jnp.einsum('gcp,gh->gphc', w.astype(F32), eye).reshape(G * P, G * SSM_GROUP)
    k = jnp.arange(SSM_CHUNK, dtype=F32)[:, None]
    lr = (ar * dt).reshape(1, G * P)
    li = (ai * dt).reshape(1, G * P)

    def power(kk):
        m = jnp.exp(kk * lr)
        return m * jnp.cos(kk * li), m * jnp.sin(kk * li)

    neg, pos, nxt = power(-k), power(k), power(k + 1.0)
    return (bd_in(bbar_r).astype(BF16), bd_in(bbar_i).astype(BF16),
            bd_out(c_re).astype(BF16), bd_out(c_im).astype(BF16),
            d_skip.astype(F32).reshape(1, G * SSM_GROUP), (*neg, *pos, *nxt))


def _merge_kernel(h_ref, att_ref, z_ref, g1_ref, wg_ref, bg_ref, wa_ref, wv_ref, wz_ref, wo_ref, g2_ref,
                  h1_ref, hn2_ref):
    D = h_ref.shape[1]
    h = h_ref[...]
    hn = _rms(h, g1_ref[...]).astype(BF16)
    gate = _sigmoid(jnp.dot(hn, wg_ref[...], preferred_element_type=F32) + bg_ref[...])
    att = jnp.concatenate([att_ref[i] for i in range(ATT_HEADS)], axis=-1)
    branch_att = jnp.dot(att, wa_ref[...], preferred_element_type=F32)
    z = z_ref[...]
    branch_ssm = (jnp.dot(z, wv_ref[...], preferred_element_type=F32)
                  * _sigmoid(jnp.dot(z, wz_ref[...], preferred_element_type=F32)))
    merged = gate[:, :D] * branch_att + gate[:, D:] * branch_ssm
    h1 = h + jnp.dot(merged.astype(BF16), wo_ref[...], preferred_element_type=F32)
    h1_ref[...] = h1
    hn2_ref[...] = _rms(h1, g2_ref[...]).T.astype(hn2_ref.dtype)


def _merge(h, att, z, g1, wg, bg, wa, wv, wz, wo, g2, tm):
    B, L, D = h.shape
    const = lambda shape: pl.BlockSpec(shape, lambda b, i: (0,) * len(shape))
    tok = lambda w: pl.BlockSpec((None, tm, w), lambda b, i: (b, i, 0))
    return pl.pallas_call(
        _merge_kernel,
        grid=(B, L // tm),
        in_specs=[tok(D),
                  pl.BlockSpec((None, ATT_HEADS, tm, ATT_HEAD_DIM), lambda b, i: (b, 0, i, 0)),
                  tok(SSM_WIDTH), const((1, D)), const((D, 2 * D)), const((1, 2 * D)),
                  const((ATT_WIDTH, D)), const((SSM_WIDTH, D)), const((SSM_WIDTH, D)),
                  const((D, D)), const((1, D))],
        out_specs=[tok(D), pl.BlockSpec((None, D, tm), lambda b, i: (b, 0, i))],
        out_shape=[jax.ShapeDtypeStruct((B, L, D), F32), jax.ShapeDtypeStruct((B, D, L), BF16)],
        compiler_params=pltpu.CompilerParams(
            dimension_semantics=("arbitrary", "arbitrary"), vmem_limit_bytes=VMEM_LIMIT),
        name="merge",
    )(h, att, z, g1, wg, bg, wa, wv, wz, wo, g2)


def _top16(s, with_rank=False):
    vals = []
    rank = jnp.full(s.shape, float(PEER_TOPK), F32) if with_rank else None
    for a in range(PEER_TOPK):
        m = jnp.max(s, axis=0, keepdims=True)
        vals.append(m)
        hit = s == m
        if with_rank:
            rank = jnp.where(hit, float(a), rank)
        s = jnp.where(hit, -jnp.inf, s)
    return vals, rank


def _peer_kernel(hnt_ref, h1_ref, wq_ref, keys_ref, u_ref, vt_ref, go_ref, o_ref,
                 e1_ref, n1_ref, rank2_ref, e2_ref, act_ref, w_ref, acc_ref):
    c = pl.program_id(2)
    tq = hnt_ref.shape[1]
    first_keys = EXPERT_CHUNK // N_KEYS

    @pl.when(c == 0)
    def _():
        acc_ref[...] = jnp.zeros_like(acc_ref)
        qt = jnp.dot(wq_ref[...], hnt_ref[...], preferred_element_type=F32).astype(BF16)
        for h in range(PEER_HEADS):
            r = 2 * h * HALF_KEY
            s1 = jnp.dot(keys_ref[2 * h], qt[r:r + HALF_KEY, :], preferred_element_type=F32)
            s2 = jnp.dot(keys_ref[2 * h + 1], qt[r + HALF_KEY:r + 2 * HALF_KEY, :],
                         preferred_element_type=F32)
            v1, _ = _top16(s1)
            v2, rank2 = _top16(s2, with_rank=True)
            v2_all = jnp.concatenate(v2, axis=0)
            cand = jnp.concatenate(
                [v1[0] + v2_all] + [v1[a] + v2_all[:8] for a in range(1, PEER_TOPK)], axis=0)
            best, _ = _top16(cand)
            tau = best[-1]
            zsum = sum(jnp.exp(b - best[0]) for b in best)
            n1 = jnp.zeros_like(s1)
            for a in range(PEER_TOPK):
                n_a = jnp.sum(jnp.where(v1[a] + v2_all >= tau, 1.0, 0.0), axis=0, keepdims=True)
                n1 = jnp.where(s1 == v1[a], n_a, n1)
            n1_ref[h] = n1
            rank2_ref[h] = rank2.astype(BF16)
            e1_ref[h] = jnp.exp(s1 - v1[0])
            e2_ref[h] = (jnp.exp(s2 - v2[0]) / zsum).astype(BF16)

    act_ref[...] = jnp.dot(u_ref[...], hnt_ref[...], preferred_element_type=F32)

    def per_first_key(ii, carry):
        i = c * first_keys + ii
        g = jnp.zeros((N_KEYS, tq), BF16)
        for h in range(PEER_HEADS):
            n1 = n1_ref[h, pl.ds(i, 1), :].astype(BF16)
            e1 = e1_ref[h, pl.ds(i, 1), :].astype(BF16)
            g = g + jnp.where(rank2_ref[h] < n1, e2_ref[h] * e1, 0.0)
        off = pl.multiple_of(ii * N_KEYS, N_KEYS)
        w_ref[pl.ds(off, N_KEYS), :] = (g.astype(F32) * _gelu(act_ref[pl.ds(off, N_KEYS), :])).astype(BF16)
        return carry

    lax.fori_loop(0, first_keys, per_first_key, 0)
    acc_ref[...] += jnp.dot(vt_ref[...], w_ref[...], preferred_element_type=F32)

    @pl.when(c == pl.num_programs(2) - 1)
    def _():
        h2 = h1_ref[...] + acc_ref[...].T
        o_ref[...] = _rms(h2, go_ref[...])


def _peer(hn2t, h1, wq_t, keys, u, v_t, g_out, tq):
    B, L, D = h1.shape
    n_experts = u.shape[0]
    tok = pl.BlockSpec((None, tq, D), lambda b, i, c: (b, i, 0))
    const = lambda shape: pl.BlockSpec(shape, lambda b, i, c: (0,) * len(shape))
    per_head = lambda dtype: pltpu.VMEM((PEER_HEADS, N_KEYS, tq), dtype)
    return pl.pallas_call(
        _peer_kernel,
        grid=(B, L // tq, n_experts // EXPERT_CHUNK),
        in_specs=[pl.BlockSpec((None, D, tq), lambda b, i, c: (b, 0, i)),
                  tok, const(wq_t.shape), const(keys.shape),
                  pl.BlockSpec((EXPERT_CHUNK, D), lambda b, i, c: (c, 0)),
                  pl.BlockSpec((D, EXPERT_CHUNK), lambda b, i, c: (0, c)),
                  const((1, D))],
        out_specs=tok,
        out_shape=jax.ShapeDtypeStruct((B, L, D), F32),
        scratch_shapes=[per_head(F32), per_head(F32), per_head(BF16), per_head(BF16),
                        pltpu.VMEM((EXPERT_CHUNK, tq), F32), pltpu.VMEM((EXPERT_CHUNK, tq), BF16),
                        pltpu.VMEM((D, tq), F32)],
        compiler_params=pltpu.CompilerParams(
            dimension_semantics=("arbitrary", "arbitrary", "arbitrary"), vmem_limit_bytes=VMEM_LIMIT),
        name="peer",
    )(hn2t, h1, wq_t, keys, u, v_t, g_out)


def kernel(x, meta_tokens, mix_norm_g, w_in, b_forget, b_gate, w_att_branch, ssm_a_re, ssm_a_im, ssm_log_dt,
           ssm_b_re, ssm_b_im, ssm_c_re, ssm_c_im, ssm_d, w_glu_val, w_glu_gate, w_out, ffn_norm_g,
           w_query, sub_keys, expert_u, expert_v, out_norm_g):
    B, S, D = x.shape
    tm = TOKEN_TILE
    L = S + N_META
    L_pad = -(-L // tm) * tm
    meta = jnp.broadcast_to(meta_tokens.astype(x.dtype)[None], (B, N_META, D))
    h = jnp.concatenate([meta, x, jnp.zeros((B, L_pad - L, D), x.dtype)], axis=1)
    row = lambda v: v.astype(F32).reshape(1, -1)

    for layer in range(w_in.shape[0]):
        wl = w_in[layer]
        o_f = 3 * ATT_WIDTH
        o_u = o_f + ATT_HEADS
        o_g = o_u + SSM_WIDTH
        w1 = jnp.concatenate(
            [wl[:, :ATT_WIDTH] * (ATT_HEAD_DIM ** -0.5), wl[:, ATT_WIDTH:o_f], wl[:, o_u:o_g],
             jnp.pad(wl[:, o_f:o_u], ((0, 0), (0, LANE - ATT_HEADS)))], axis=1).astype(BF16)
        bf = jnp.pad(b_forget[layer].astype(F32), (0, LANE - ATT_HEADS)).reshape(1, LANE)
        q, k, v, u = _inproj(h, row(mix_norm_g[layer]), w1, bf, tm)
        att = _attn(q, k, v, tm)
        bre, bim, cre, cim, dsk, tabs = _ssm_params(
            ssm_a_re[layer], ssm_a_im[layer], ssm_log_dt[layer], ssm_b_re[layer], ssm_b_im[layer],
            ssm_c_re[layer], ssm_c_im[layer], ssm_d[layer])
        z = _ssm(u, bre, bim, cre, cim, dsk, tabs)
        h1, hn2 = _merge(h, att, z, row(mix_norm_g[layer]), wl[:, o_g:].astype(BF16), row(b_gate[layer]),
                         w_att_branch[layer].astype(BF16), w_glu_val[layer].astype(BF16),
                         w_glu_gate[layer].astype(BF16), w_out[layer].astype(BF16), row(ffn_norm_g[layer]), tm)
        assert layer == w_in.shape[0] - 1, "only the last layer's PEER kernel fuses the output norm"
        keys = sub_keys[layer].astype(BF16).reshape(2 * PEER_HEADS, N_KEYS, HALF_KEY)
        h = _peer(hn2, h1, w_query[layer].T.astype(BF16), keys, expert_u[layer].astype(BF16),
                  expert_v[layer].T.astype(BF16), row(out_norm_g), tm)
    return h[:, N_META:N_META + S, :]
```

```python
import jax
import jax.numpy as jnp
import numpy as np
from jax import lax
from jax.experimental import pallas as pl
from jax.experimental.pallas import tpu as pltpu

F32 = jnp.float32
BF16 = jnp.bfloat16

N_META = 16
ATT_HEADS = 8
ATT_HEAD_DIM = 64
ATT_WIDTH = ATT_HEADS * ATT_HEAD_DIM
SSM_WIDTH = 512
SSM_GROUP = 16
SSM_GROUPS = SSM_WIDTH // SSM_GROUP
SSM_STATE = 64
SSM_LANES = SSM_GROUPS * SSM_STATE
PEER_HEADS = 8
N_KEYS = 128
PEER_TOPK = 16
HALF_KEY = 128
RMS_EPS = 1e-6
NEG_INF = -1e30

LANE = 128
MXU_TILE = 256
TOKEN_TILE = 512
PREFIX = TOKEN_TILE
META_TILE = 128
HEADS_PER_STEP = 2
SSM_CHUNK = 256
EXPERT_CHUNK = 2048
VMEM_LIMIT = 56 * 1024 * 1024


def _rms(x, g):
    return x * lax.rsqrt(jnp.mean(x * x, axis=-1, keepdims=True) + RMS_EPS) * g


def _gelu(x):
    return 0.5 * x * (1.0 + lax.erf(x * (2.0 ** -0.5)))


def _sigmoid(x):
    return 1.0 / (1.0 + jnp.exp(-x))


def _log_sigmoid(x):
    return jnp.minimum(x, 0.0) - jnp.log1p(jnp.exp(-jnp.abs(x)))


def _split3(c):
    hi = c.astype(BF16).astype(F32)
    r = c - hi
    mid = r.astype(BF16).astype(F32)
    lo = (r - mid).astype(BF16).astype(F32)
    return hi, mid, lo


def _inproj_kernel(x_ref, pre_ref, g_ref, w_ref, bf_ref, q_ref, k_ref, v_ref, u_ref, carry_ref):
    tm = x_ref.shape[0]
    i = pl.program_id(1)

    @pl.when(i == 0)
    def _():
        carry_ref[...] = jnp.zeros_like(carry_ref)

    h = jnp.where(i == 0, pre_ref[...], x_ref[...])
    hn = _rms(h, g_ref[...]).astype(BF16)
    proj = jnp.dot(hn, w_ref[...], preferred_element_type=F32)
    u_ref[...] = proj[:, 3 * ATT_WIDTH:3 * ATT_WIDTH + SSM_WIDTH].T

    log_f = _log_sigmoid(proj[:, 3 * ATT_WIDTH + SSM_WIDTH:] + bf_ref[...])
    row = lax.broadcasted_iota(jnp.int32, (tm, tm), 0)
    col = lax.broadcasted_iota(jnp.int32, (tm, tm), 1)
    tri = (row >= col).astype(F32)
    c = jnp.dot(tri, log_f, preferred_element_type=F32, precision=lax.Precision.HIGHEST) + carry_ref[...]
    carry_ref[...] = c[tm - 1:tm, :]
    hi, mid, lo = _split3(c)

    lane = lax.broadcasted_iota(jnp.int32, (tm, ATT_HEAD_DIM), 1)
    ones3 = jnp.where(lane < 3, 1.0, 0.0)
    v_ext = jnp.where(lane == 0, 1.0, 0.0)
    for hd in range(ATT_HEADS):
        ch, cm, cl = hi[:, hd:hd + 1], mid[:, hd:hd + 1], lo[:, hd:hd + 1]
        cvec = jnp.where(lane == 0, ch, jnp.where(lane == 1, cm, jnp.where(lane == 2, cl, 0.0)))
        q_ext = cvec + jnp.where((lane >= 3) & (lane < 6), 1.0, 0.0)
        k_ext = ones3 - jnp.where(lane == 3, ch, jnp.where(lane == 4, cm, jnp.where(lane == 5, cl, 0.0)))
        lo_c, hi_c = hd * ATT_HEAD_DIM, (hd + 1) * ATT_HEAD_DIM
        q_ref[hd] = jnp.concatenate([proj[:, lo_c:hi_c], q_ext], axis=-1).T.astype(BF16)
        k_ref[hd] = jnp.concatenate(
            [proj[:, ATT_WIDTH + lo_c:ATT_WIDTH + hi_c].astype(BF16), k_ext.astype(BF16)], axis=-1)
        v_ref[hd] = jnp.concatenate(
            [proj[:, 2 * ATT_WIDTH + lo_c:2 * ATT_WIDTH + hi_c], v_ext], axis=-1).T.astype(BF16)


def _inproj(x, pre, g, w, bf, tm):
    B, S, D = x.shape
    L = PREFIX + S
    ncol = w.shape[1]
    head_shape = jax.ShapeDtypeStruct((B, ATT_HEADS, L, LANE), BF16)
    head_spec = pl.BlockSpec((None, ATT_HEADS, tm, LANE), lambda b, i: (b, 0, i, 0))
    head_t_shape = jax.ShapeDtypeStruct((B, ATT_HEADS, LANE, L), BF16)
    head_t_spec = pl.BlockSpec((None, ATT_HEADS, LANE, tm), lambda b, i: (b, 0, 0, i))
    const = lambda shape: pl.BlockSpec(shape, lambda b, i: (0, 0))
    return pl.pallas_call(
        _inproj_kernel,
        grid=(B, L // tm),
        in_specs=[pl.BlockSpec((None, tm, D), lambda b, i: (b, jnp.maximum(i - 1, 0), 0)),
                  const((tm, D)), const((1, D)), const((D, ncol)), const((1, LANE))],
        out_specs=[head_t_spec, head_spec, head_t_spec,
                   pl.BlockSpec((None, SSM_WIDTH, tm), lambda b, i: (b, 0, i))],
        out_shape=[head_t_shape, head_shape, head_t_shape,
                   jax.ShapeDtypeStruct((B, SSM_WIDTH, L), F32)],
        scratch_shapes=[pltpu.VMEM((1, LANE), F32)],
        compiler_params=pltpu.CompilerParams(
            dimension_semantics=("arbitrary", "arbitrary"), vmem_limit_bytes=VMEM_LIMIT),
        name="inproj",
    )(x, pre, g, w, bf)


def _attn_kernel(q_ref, k_ref, v_ref, o_ref, m_ref, acc_ref, alpha_ref, p_ref):
    qi = pl.program_id(2)
    tq = q_ref.shape[2]
    tk = tq
    heads = range(HEADS_PER_STEP)

    def scores(hh, start, size):
        return jnp.dot(k_ref[hh, pl.ds(start, size), :], q_ref[hh], preferred_element_type=F32)

    def tile_start(j):
        return pl.multiple_of(PREFIX + j * tk, tk)

    def softmax_tile(hh, s):
        m_prev = m_ref[hh]
        m_new = jnp.maximum(m_prev, jnp.max(s, axis=0, keepdims=True))
        m_ref[hh] = m_new
        alpha_ref[hh] = jnp.exp(m_prev - m_new)
        p_ref[hh] = jnp.exp(s - m_new).astype(BF16)

    def flush(hh, j):
        acc = alpha_ref[hh] * acc_ref[hh] + jnp.dot(v_ref[hh, :, pl.ds(tile_start(j), tk)], p_ref[hh],
                                                    preferred_element_type=F32)
        acc_ref[hh] = acc
        return acc

    key = lax.broadcasted_iota(jnp.int32, (META_TILE, tq), 0)
    for hh in heads:
        s = jnp.where(key >= META_TILE - N_META, scores(hh, PREFIX - META_TILE, META_TILE), NEG_INF)
        m0 = jnp.max(s, axis=0, keepdims=True)
        p0 = jnp.exp(s - m0).astype(BF16)
        acc_ref[hh] = jnp.dot(v_ref[hh, :, PREFIX - META_TILE:PREFIX], p0, preferred_element_type=F32)
        m_ref[hh] = m0

    key = lax.broadcasted_iota(jnp.int32, (tk, tq), 0)
    qry = lax.broadcasted_iota(jnp.int32, (tk, tq), 1)
    for hh in heads:
        softmax_tile(hh, jnp.where(key <= qry, scores(hh, tile_start(qi), tk), NEG_INF))

    def one_tile(j, pending):
        for hh in heads:
            flush(hh, pending)
            softmax_tile(hh, scores(hh, tile_start(j), tk))
        return j

    def two_tiles(jj, pending):
        return one_tile(2 * jj + 1, one_tile(2 * jj, pending))

    pending = lax.fori_loop(0, qi // 2, two_tiles, qi)
    pending = lax.fori_loop(0, qi % 2, lambda _, pend: one_tile(qi - 1, pend), pending)

    outs = []
    for hh in heads:
        acc = flush(hh, pending).T
        outs.append(acc[:, :ATT_HEAD_DIM] / acc[:, ATT_HEAD_DIM:ATT_HEAD_DIM + 1])
    o_ref[...] = jnp.concatenate(outs, axis=-1).astype(o_ref.dtype)


def _attn(qt, k, vt, tq):
    B, H, L, _ = k.shape
    first = PREFIX // tq
    hp = HEADS_PER_STEP
    return pl.pallas_call(
        _attn_kernel,
        grid=(B, H // hp, (L - PREFIX) // tq),
        in_specs=[pl.BlockSpec((None, hp, LANE, tq), lambda b, h, i: (b, h, 0, i + first)),
                  pl.BlockSpec((None, hp, L, LANE), lambda b, h, i: (b, h, 0, 0)),
                  pl.BlockSpec((None, hp, LANE, L), lambda b, h, i: (b, h, 0, 0))],
        out_specs=pl.BlockSpec((None, tq, hp * ATT_HEAD_DIM), lambda b, h, i: (b, i, h)),
        out_shape=jax.ShapeDtypeStruct((B, L - PREFIX, H * ATT_HEAD_DIM), BF16),
        scratch_shapes=[pltpu.VMEM((hp, 1, tq), F32), pltpu.VMEM((hp, LANE, tq), F32),
                        pltpu.VMEM((hp, 1, tq), F32), pltpu.VMEM((hp, tq, tq), BF16)],
        compiler_params=pltpu.CompilerParams(
            dimension_semantics=("arbitrary", "arbitrary", "arbitrary"), vmem_limit_bytes=VMEM_LIMIT),
        name="attn",
    )(qt, k, vt)


def _ssm_kernel(u_ref, bre_ref, bim_ref, cre_ref, cim_ref, d_ref, negr_ref, negi_ref, posr_ref, posi_ref,
                nxtr_ref, nxti_ref, tri_ref, z_ref, cr_ref, ci_ref):
    tc = u_ref.shape[1]

    @pl.when(pl.program_id(1) == 0)
    def _():
        cr_ref[...] = jnp.zeros_like(cr_ref)
        ci_ref[...] = jnp.zeros_like(ci_ref)

    u = u_ref[...]
    ub = u.astype(BF16)
    n_state, n_chan = bre_ref.shape

    def block_diag_dot(w_ref, rhs, blocks):
        rb, cb = w_ref.shape[0] // blocks, w_ref.shape[1] // blocks
        return jnp.concatenate(
            [jnp.dot(w_ref[i * rb:(i + 1) * rb, i * cb:(i + 1) * cb], rhs[i * cb:(i + 1) * cb, :],
                     preferred_element_type=F32) for i in range(blocks)], axis=0)

    in_blocks = n_chan // MXU_TILE
    out_blocks = n_state // MXU_TILE
    bur = block_diag_dot(bre_ref, ub, in_blocks)
    bui = block_diag_dot(bim_ref, ub, in_blocks)
    nr, ni = negr_ref[...], negi_ref[...]
    tri = tri_ref[...]

    def prefix_sum(z):
        hi = z.astype(BF16)
        lo = (z - hi.astype(F32)).astype(BF16)
        return (jnp.dot(hi, tri, preferred_element_type=F32) + jnp.dot(lo, tri, preferred_element_type=F32))

    reps = tc // LANE
    tr = prefix_sum(nr * bur - ni * bui) + jnp.concatenate([cr_ref[...]] * reps, axis=1)
    ti = prefix_sum(nr * bui + ni * bur) + jnp.concatenate([ci_ref[...]] * reps, axis=1)
    pr, pi = posr_ref[...], posi_ref[...]
    xr = pr * tr - pi * ti
    xi = pr * ti + pi * tr
    lr, li = tr[:, tc - LANE:], ti[:, tc - LANE:]
    qr, qi = nxtr_ref[...], nxti_ref[...]
    cr_ref[...] = jnp.broadcast_to((qr * lr - qi * li)[:, LANE - 1:], cr_ref.shape)
    ci_ref[...] = jnp.broadcast_to((qr * li + qi * lr)[:, LANE - 1:], ci_ref.shape)
    y = (block_diag_dot(cre_ref, xr.astype(BF16), out_blocks)
         - block_diag_dot(cim_ref, xi.astype(BF16), out_blocks)
         + d_ref[...] * u)
    z_ref[...] = _gelu(y).T.astype(z_ref.dtype)


def _ssm(ut, bre, bim, cre, cim, d, tabs, tri):
    B, W, L = ut.shape
    tc = SSM_CHUNK
    first = (PREFIX - tc) // tc
    n_real = (L - PREFIX) // tc
    const = lambda shape: pl.BlockSpec(shape, lambda b, i: (0, 0), pipeline_mode=pl.Buffered(1))
    return pl.pallas_call(
        _ssm_kernel,
        grid=(B, 1 + n_real),
        in_specs=[pl.BlockSpec((None, W, tc), lambda b, i: (b, 0, i + first)),
                  const((SSM_LANES, W)), const((SSM_LANES, W)),
                  const((W, SSM_LANES)), const((W, SSM_LANES)), const((W, tc))]
                 + [const((SSM_LANES, tc))] * 4 + [const((SSM_LANES, LANE))] * 2 + [const((tc, tc))],
        out_specs=pl.BlockSpec((None, tc, W), lambda b, i: (b, jnp.maximum(i - 1, 0), 0)),
        out_shape=jax.ShapeDtypeStruct((B, L - PREFIX, W), BF16),
        scratch_shapes=[pltpu.VMEM((SSM_LANES, LANE), F32), pltpu.VMEM((SSM_LANES, LANE), F32)],
        compiler_params=pltpu.CompilerParams(
            dimension_semantics=("arbitrary", "arbitrary"), vmem_limit_bytes=VMEM_LIMIT),
        name="ssm",
    )(ut, bre, bim, cre, cim, d, *tabs, tri)


def _ssm_params(a_re, a_im, log_dt, b_re, b_im, c_re, c_im, d_skip):
    G, P = a_re.shape
    dt = jnp.exp(log_dt.astype(F32))[:, None]
    ar, ai = a_re.astype(F32), a_im.astype(F32)
    mag = jnp.exp(ar * dt)
    abar_r = mag * jnp.cos(ai * dt)
    abar_i = mag * jnp.sin(ai * dt)
    num_r, num_i = abar_r - 1.0, abar_i
    den = ar * ar + ai * ai
    coef_r = (num_r * ar + num_i * ai) / den
    coef_i = (num_i * ar - num_r * ai) / den
    br, bi = b_re.astype(F32), b_im.astype(F32)
    bbar_r = coef_r[..., None] * br - coef_i[..., None] * bi
    bbar_i = coef_r[..., None] * bi + coef_i[..., None] * br
    eye = jnp.eye(G, dtype=F32)
    bd_in = lambda w: jnp.einsum('gpc,gh->gphc', w, eye).reshape(G * P, G * SSM_GROUP)
    bd_out = lambda w: jnp.einsum('gcp,gh->gchp', w.astype(F32), eye).reshape(G * SSM_GROUP, G * P)
    k = jnp.arange(SSM_CHUNK, dtype=F32)[None, :]
    lr = (ar * dt).reshape(G * P, 1)
    li = (ai * dt).reshape(G * P, 1)

    def power(kk):
        m = jnp.exp(kk * lr)
        return m * jnp.cos(kk * li), m * jnp.sin(kk * li)

    neg, pos, nxt = power(-k), power(k), power(k[:, SSM_CHUNK - LANE:] + 1.0)
    d_tab = jnp.broadcast_to(d_skip.astype(F32).reshape(G * SSM_GROUP, 1), (G * SSM_GROUP, SSM_CHUNK))
    tri = jnp.triu(jnp.ones((SSM_CHUNK, SSM_CHUNK), F32)).astype(BF16)
    return (bd_in(bbar_r).astype(BF16), bd_in(bbar_i).astype(BF16),
            bd_out(c_re).astype(BF16), bd_out(c_im).astype(BF16), d_tab, (*neg, *pos, *nxt), tri)


def _merge_kernel(h_ref, att_ref, z_ref, g1_ref, wg_ref, bg_ref, wa_ref, wv_ref, wz_ref, wo_ref, g2_ref,
                  h1_ref, hn2_ref):
    D = h_ref.shape[1]
    h = h_ref[...]
    hn = _rms(h, g1_ref[...]).astype(BF16)
    gate = _sigmoid(jnp.dot(hn, wg_ref[...], preferred_element_type=F32) + bg_ref[...])
    branch_att = jnp.dot(att_ref[...], wa_ref[...], preferred_element_type=F32)
    z = z_ref[...]
    branch_ssm = (jnp.dot(z, wv_ref[...], preferred_element_type=F32)
                  * _sigmoid(jnp.dot(z, wz_ref[...], preferred_element_type=F32)))
    merged = gate[:, :D] * branch_att + gate[:, D:] * branch_ssm
    h1 = h + jnp.dot(merged.astype(BF16), wo_ref[...], preferred_element_type=F32)
    h1_ref[...] = h1
    hn2_ref[...] = _rms(h1, g2_ref[...]).T.astype(hn2_ref.dtype)


def _merge(h, att, z, g1, wg, bg, wa, wv, wz, wo, g2, tm):
    B, S, D = h.shape
    const = lambda shape: pl.BlockSpec(shape, lambda b, i: (0,) * len(shape))
    tok = lambda w: pl.BlockSpec((None, tm, w), lambda b, i: (b, i, 0))
    return pl.pallas_call(
        _merge_kernel,
        grid=(B, S // tm),
        in_specs=[tok(D), tok(ATT_WIDTH), tok(SSM_WIDTH), const((1, D)), const((D, 2 * D)), const((1, 2 * D)),
                  const((ATT_WIDTH, D)), const((SSM_WIDTH, D)), const((SSM_WIDTH, D)),
                  const((D, D)), const((1, D))],
        out_specs=[tok(D), pl.BlockSpec((None, D, tm), lambda b, i: (b, 0, i))],
        out_shape=[jax.ShapeDtypeStruct((B, S, D), F32), jax.ShapeDtypeStruct((B, D, S), BF16)],
        compiler_params=pltpu.CompilerParams(
            dimension_semantics=("arbitrary", "arbitrary"), vmem_limit_bytes=VMEM_LIMIT),
        name="merge",
    )(h, att, z, g1, wg, bg, wa, wv, wz, wo, g2)


def _top16(s, with_rank=False):
    vals = []
    rank = jnp.full(s.shape, float(PEER_TOPK), F32) if with_rank else None
    for a in range(PEER_TOPK):
        m = jnp.max(s, axis=0, keepdims=True)
        vals.append(m)
        hit = s == m
        if with_rank:
            rank = jnp.where(hit, float(a), rank)
        s = jnp.where(hit, -jnp.inf, s)
    return vals, rank


def _peer_kernel(hnt_ref, h1_ref, wq_ref, keys_ref, u_ref, vt_ref, go_ref, o_ref,
                 e1_ref, n1_ref, rank2_ref, e2_ref, acc_ref):
    c = pl.program_id(2)
    tq = hnt_ref.shape[1]
    first_keys = EXPERT_CHUNK // N_KEYS

    @pl.when(c == 0)
    def _():
        acc_ref[...] = jnp.zeros_like(acc_ref)
        qt = jnp.dot(wq_ref[...], hnt_ref[...], preferred_element_type=F32).astype(BF16)
        for h in range(PEER_HEADS):
            r = 2 * h * HALF_KEY
            s1 = jnp.dot(keys_ref[2 * h], qt[r:r + HALF_KEY, :], preferred_element_type=F32)
            s2 = jnp.dot(keys_ref[2 * h + 1], qt[r + HALF_KEY:r + 2 * HALF_KEY, :],
                         preferred_element_type=F32)
            v1, _ = _top16(s1)
            v2, rank2 = _top16(s2, with_rank=True)
            v2_all = jnp.concatenate(v2, axis=0)
            cand = jnp.concatenate(
                [v1[0] + v2_all] + [v1[a] + v2_all[:8] for a in range(1, PEER_TOPK)], axis=0)
            best, _ = _top16(cand)
            tau = best[-1]
            zsum = sum(jnp.exp(b - best[0]) for b in best)
            n1 = jnp.zeros_like(s1)
            for a in range(PEER_TOPK):
                n_a = jnp.sum(jnp.where(v1[a] + v2_all >= tau, 1.0, 0.0), axis=0, keepdims=True)
                n1 = jnp.where(s1 == v1[a], n_a, n1)
            n1_ref[h] = n1
            rank2_ref[h] = rank2.astype(BF16)
            e1_ref[h] = jnp.exp(s1 - v1[0])
            e2_ref[h] = (jnp.exp(s2 - v2[0]) / zsum).astype(BF16)

    hnt = hnt_ref[...]
    ws = []
    for ii in range(first_keys):
        i = c * first_keys + ii
        act = jnp.dot(u_ref[ii * N_KEYS:(ii + 1) * N_KEYS, :], hnt, preferred_element_type=F32)
        g = jnp.zeros((N_KEYS, tq), BF16)
        for h in range(PEER_HEADS):
            n1 = n1_ref[h, pl.ds(i, 1), :].astype(BF16)
            e1 = e1_ref[h, pl.ds(i, 1), :].astype(BF16)
            g = g + jnp.where(rank2_ref[h] < n1, e2_ref[h] * e1, 0.0)
        ws.append(g * _gelu(act.astype(BF16)))
    acc_ref[...] += jnp.dot(vt_ref[...], jnp.concatenate(ws, axis=0), preferred_element_type=F32)

    @pl.when(c == pl.num_programs(2) - 1)
    def _():
        h2 = h1_ref[...] + acc_ref[...].T
        o_ref[...] = _rms(h2, go_ref[...])


def _peer(hn2t, h1, wq_t, keys, u, v_t, g_out, tq):
    B, S, D = h1.shape
    n_experts = u.shape[0]
    tok = pl.BlockSpec((None, tq, D), lambda b, i, c: (b, i, 0))
    const = lambda shape: pl.BlockSpec(shape, lambda b, i, c: (0,) * len(shape))
    per_head = lambda dtype: pltpu.VMEM((PEER_HEADS, N_KEYS, tq), dtype)
    return pl.pallas_call(
        _peer_kernel,
        grid=(B, S // tq, n_experts // EXPERT_CHUNK),
        in_specs=[pl.BlockSpec((None, D, tq), lambda b, i, c: (b, 0, i)),
                  tok, const(wq_t.shape), const(keys.shape),
                  pl.BlockSpec((EXPERT_CHUNK, D), lambda b, i, c: (c, 0)),
                  pl.BlockSpec((D, EXPERT_CHUNK), lambda b, i, c: (0, c)),
                  const((1, D))],
        out_specs=tok,
        out_shape=jax.ShapeDtypeStruct((B, S, D), F32),
        scratch_shapes=[per_head(F32), per_head(F32), per_head(BF16), per_head(BF16),
                        pltpu.VMEM((D, tq), F32)],
        compiler_params=pltpu.CompilerParams(
            dimension_semantics=("arbitrary", "arbitrary", "arbitrary"), vmem_limit_bytes=VMEM_LIMIT),
        name="peer",
    )(hn2t, h1, wq_t, keys, u, v_t, g_out)


def kernel(x, meta_tokens, mix_norm_g, w_in, b_forget, b_gate, w_att_branch, ssm_a_re, ssm_a_im, ssm_log_dt,
           ssm_b_re, ssm_b_im, ssm_c_re, ssm_c_im, ssm_d, w_glu_val, w_glu_gate, w_out, ffn_norm_g,
           w_query, sub_keys, expert_u, expert_v, out_norm_g):
    B, S, D = x.shape
    tm = TOKEN_TILE
    assert S % tm == 0 and w_in.shape[0] == 1, "one layer over a whole number of row tiles"
    layer = 0
    row = lambda v: v.astype(F32).reshape(1, -1)
    pre = jnp.concatenate([jnp.zeros((PREFIX - N_META, D), x.dtype), meta_tokens.astype(x.dtype)], axis=0)

    wl = w_in[layer]
    o_f = 3 * ATT_WIDTH
    o_u = o_f + ATT_HEADS
    o_g = o_u + SSM_WIDTH
    w1 = jnp.concatenate(
        [wl[:, :ATT_WIDTH] * (ATT_HEAD_DIM ** -0.5), wl[:, ATT_WIDTH:o_f], wl[:, o_u:o_g],
         jnp.pad(wl[:, o_f:o_u], ((0, 0), (0, LANE - ATT_HEADS)))], axis=1).astype(BF16)
    bf = jnp.pad(b_forget[layer].astype(F32), (0, LANE - ATT_HEADS)).reshape(1, LANE)
    qt, k, vt, ut = _inproj(x, pre, row(mix_norm_g[layer]), w1, bf, tm)
    att = _attn(qt, k, vt, tm)
    z = _ssm(ut, *_ssm_params(
        ssm_a_re[layer], ssm_a_im[layer], ssm_log_dt[layer], ssm_b_re[layer], ssm_b_im[layer],
        ssm_c_re[layer], ssm_c_im[layer], ssm_d[layer]))
    h1, hn2t = _merge(x, att, z, row(mix_norm_g[layer]), wl[:, o_g:].astype(BF16), row(b_gate[layer]),
                      w_att_branch[layer].astype(BF16), w_glu_val[layer].astype(BF16),
                      w_glu_gate[layer].astype(BF16), w_out[layer].astype(BF16), row(ffn_norm_g[layer]), tm)
    keys = sub_keys[layer].astype(BF16).reshape(2 * PEER_HEADS, N_KEYS, HALF_KEY)
    return _peer(hn2t, h1, w_query[layer].T.astype(BF16), keys, expert_u[layer].astype(BF16),
                 expert_v[layer].T.astype(BF16), row(out_norm_g), tm)
```

```python
import jax
import jax.numpy as jnp
import numpy as np
from jax import lax
from jax.experimental import pallas as pl
from jax.experimental.pallas import tpu as pltpu

F32 = jnp.float32
BF16 = jnp.bfloat16

N_META = 16
ATT_HEADS = 8
ATT_HEAD_DIM = 64
ATT_WIDTH = ATT_HEADS * ATT_HEAD_DIM
SSM_WIDTH = 512
SSM_GROUP = 16
SSM_GROUPS = SSM_WIDTH // SSM_GROUP
SSM_STATE = 64
SSM_LANES = SSM_GROUPS * SSM_STATE
PEER_HEADS = 8
N_KEYS = 128
PEER_TOPK = 16
HALF_KEY = 128
RMS_EPS = 1e-6
NEG_INF = -1e30

LANE = 128
MXU_TILE = 256
TOKEN_TILE = 512
PREFIX = TOKEN_TILE
META_TILE = 128
HEADS_PER_STEP = 2
KEY_TILES_PER_BODY = 2
SSM_CHUNK = 256
EXPERT_CHUNK = 2048
VMEM_LIMIT = 56 * 1024 * 1024


def _rms(x, g):
    return x * lax.rsqrt(jnp.mean(x * x, axis=-1, keepdims=True) + RMS_EPS) * g


def _gelu(x):
    return 0.5 * x * (1.0 + lax.erf(x * (2.0 ** -0.5)))


def _sigmoid(x):
    return 1.0 / (1.0 + jnp.exp(-x))


def _log_sigmoid(x):
    return jnp.minimum(x, 0.0) - jnp.log1p(jnp.exp(-jnp.abs(x)))


def _split3(c):
    hi = c.astype(BF16).astype(F32)
    r = c - hi
    mid = r.astype(BF16).astype(F32)
    lo = (r - mid).astype(BF16).astype(F32)
    return hi, mid, lo


def _inproj_kernel(x_ref, pre_ref, g_ref, wt_ref, w_ref, bf_ref, q_ref, k_ref, v_ref, u_ref, carry_ref):
    tm = x_ref.shape[0]
    i = pl.program_id(1)

    @pl.when(i == 0)
    def _():
        carry_ref[...] = jnp.zeros_like(carry_ref)

    h = jnp.where(i == 0, pre_ref[...], x_ref[...])
    hn = _rms(h, g_ref[...]).astype(BF16)
    proj_t = lax.dot_general(wt_ref[...], hn, (((1,), (1,)), ((), ())), preferred_element_type=F32)
    proj = jnp.dot(hn, w_ref[...], preferred_element_type=F32)
    u_ref[...] = proj_t[2 * ATT_WIDTH:, :]

    log_f = _log_sigmoid(proj[:, ATT_WIDTH:] + bf_ref[...])
    row = lax.broadcasted_iota(jnp.int32, (tm, tm), 0)
    col = lax.broadcasted_iota(jnp.int32, (tm, tm), 1)
    tri = jnp.where(row >= col, 1.0, 0.0).astype(BF16)
    c = carry_ref[...] + sum(jnp.dot(tri, piece.astype(BF16), preferred_element_type=F32)
                             for piece in _split3(log_f))
    carry_ref[...] = c[tm - 1:tm, :]
    hi, mid, lo = _split3(c)
    hi_t, mid_t, lo_t = _split3(c.T)

    lane = lax.broadcasted_iota(jnp.int32, (tm, ATT_HEAD_DIM), 1)
    sub = lax.broadcasted_iota(jnp.int32, (ATT_HEAD_DIM, tm), 0)
    ones3 = jnp.where(lane < 3, 1.0, 0.0)
    ones3_t = jnp.where((sub >= 3) & (sub < 6), 1.0, 0.0)
    v_ext_t = jnp.where(sub == 0, 1.0, 0.0)
    for hd in range(ATT_HEADS):
        lo_c, hi_c = hd * ATT_HEAD_DIM, (hd + 1) * ATT_HEAD_DIM
        q_ext_t = ones3_t + jnp.where(sub == 0, hi_t[hd:hd + 1], jnp.where(
            sub == 1, mid_t[hd:hd + 1], jnp.where(sub == 2, lo_t[hd:hd + 1], 0.0)))
        q_ref[hd] = jnp.concatenate([proj_t[lo_c:hi_c, :], q_ext_t], axis=0).astype(BF16)
        v_ref[hd] = jnp.concatenate([proj_t[ATT_WIDTH + lo_c:ATT_WIDTH + hi_c, :], v_ext_t], axis=0).astype(BF16)
        ch, cm, cl = hi[:, hd:hd + 1], mid[:, hd:hd + 1], lo[:, hd:hd + 1]
        k_ext = ones3 - jnp.where(lane == 3, ch, jnp.where(lane == 4, cm, jnp.where(lane == 5, cl, 0.0)))
        k_ref[hd] = jnp.concatenate([proj[:, lo_c:hi_c].astype(BF16), k_ext.astype(BF16)], axis=-1)


def _inproj(x, pre, g, wt, w, bf, tm):
    B, S, D = x.shape
    L = PREFIX + S
    ncol = w.shape[1]
    head_shape = jax.ShapeDtypeStruct((B, ATT_HEADS, L, LANE), BF16)
    head_spec = pl.BlockSpec((None, ATT_HEADS, tm, LANE), lambda b, i: (b, 0, i, 0))
    head_t_shape = jax.ShapeDtypeStruct((B, ATT_HEADS, LANE, L), BF16)
    head_t_spec = pl.BlockSpec((None, ATT_HEADS, LANE, tm), lambda b, i: (b, 0, 0, i))
    const = lambda shape: pl.BlockSpec(shape, lambda b, i: (0, 0))
    return pl.pallas_call(
        _inproj_kernel,
        grid=(B, L // tm),
        in_specs=[pl.BlockSpec((None, tm, D), lambda b, i: (b, jnp.maximum(i - 1, 0), 0)),
                  const((tm, D)), const((1, D)), const(wt.shape), const((D, ncol)), const((1, LANE))],
        out_specs=[head_t_spec, head_spec, head_t_spec,
                   pl.BlockSpec((None, SSM_WIDTH, tm), lambda b, i: (b, 0, i))],
        out_shape=[head_t_shape, head_shape, head_t_shape,
                   jax.ShapeDtypeStruct((B, SSM_WIDTH, L), F32)],
        scratch_shapes=[pltpu.VMEM((1, LANE), F32)],
        compiler_params=pltpu.CompilerParams(
            dimension_semantics=("arbitrary", "arbitrary"), vmem_limit_bytes=VMEM_LIMIT),
        name="inproj",
    )(x, pre, g, wt, w, bf)


def _attn_kernel(q_ref, k_ref, v_ref, o_ref, m_ref, acc_ref, alpha_ref, p_ref):
    qi = pl.program_id(2)
    tq = q_ref.shape[2]
    tk = tq
    heads = range(HEADS_PER_STEP)

    def scores(hh, start, size):
        return jnp.dot(k_ref[hh, pl.ds(start, size), :], q_ref[hh], preferred_element_type=F32)

    def tile_start(j):
        return pl.multiple_of(PREFIX + j * tk, tk)

    def softmax_tile(hh, s):
        m_prev = m_ref[hh]
        m_new = jnp.maximum(m_prev, jnp.max(s, axis=0, keepdims=True))
        m_ref[hh] = m_new
        alpha_ref[hh] = jnp.exp(m_prev - m_new)
        p_ref[hh] = jnp.exp(s - m_new).astype(BF16)

    def flush(hh, j):
        acc = alpha_ref[hh] * acc_ref[hh] + jnp.dot(v_ref[hh, :, pl.ds(tile_start(j), tk)], p_ref[hh],
                                                    preferred_element_type=F32)
        acc_ref[hh] = acc
        return acc

    key = lax.broadcasted_iota(jnp.int32, (META_TILE, tq), 0)
    for hh in heads:
        s = jnp.where(key >= META_TILE - N_META, scores(hh, PREFIX - META_TILE, META_TILE), NEG_INF)
        m0 = jnp.max(s, axis=0, keepdims=True)
        p0 = jnp.exp(s - m0).astype(BF16)
        acc_ref[hh] = jnp.dot(v_ref[hh, :, PREFIX - META_TILE:PREFIX], p0, preferred_element_type=F32)
        m_ref[hh] = m0

    key = lax.broadcasted_iota(jnp.int32, (tk, tq), 0)
    qry = lax.broadcasted_iota(jnp.int32, (tk, tq), 1)
    for hh in heads:
        softmax_tile(hh, jnp.where(key <= qry, scores(hh, tile_start(qi), tk), NEG_INF))

    def one_tile(j, pending):
        for hh in heads:
            flush(hh, pending)
            softmax_tile(hh, scores(hh, tile_start(j), tk))
        return j

    def unrolled(jj, pending):
        for t in range(KEY_TILES_PER_BODY):
            pending = one_tile(KEY_TILES_PER_BODY * jj + t, pending)
        return pending

    rest = qi % KEY_TILES_PER_BODY
    pending = lax.fori_loop(0, qi // KEY_TILES_PER_BODY, unrolled, qi)
    pending = lax.fori_loop(0, rest, lambda r, pend: one_tile(qi - rest + r, pend), pending)

    outs = []
    for hh in heads:
        acc = flush(hh, pending).T
        outs.append(acc[:, :ATT_HEAD_DIM] / acc[:, ATT_HEAD_DIM:ATT_HEAD_DIM + 1])
    o_ref[...] = jnp.concatenate(outs, axis=-1).astype(o_ref.dtype)


def _attn(qt, k, vt, tq):
    B, H, L, _ = k.shape
    first = PREFIX // tq
    hp = HEADS_PER_STEP
    return pl.pallas_call(
        _attn_kernel,
        grid=(B, H // hp, (L - PREFIX) // tq),
        in_specs=[pl.BlockSpec((None, hp, LANE, tq), lambda b, h, i: (b, h, 0, i + first)),
                  pl.BlockSpec((None, hp, L, LANE), lambda b, h, i: (b, h, 0, 0)),
                  pl.BlockSpec((None, hp, LANE, L), lambda b, h, i: (b, h, 0, 0))],
        out_specs=pl.BlockSpec((None, tq, hp * ATT_HEAD_DIM), lambda b, h, i: (b, i, h)),
        out_shape=jax.ShapeDtypeStruct((B, L - PREFIX, H * ATT_HEAD_DIM), BF16),
        scratch_shapes=[pltpu.VMEM((hp, 1, tq), F32), pltpu.VMEM((hp, LANE, tq), F32),
                        pltpu.VMEM((hp, 1, tq), F32), pltpu.VMEM((hp, tq, tq), BF16)],
        compiler_params=pltpu.CompilerParams(
            dimension_semantics=("arbitrary", "arbitrary", "arbitrary"), vmem_limit_bytes=VMEM_LIMIT),
        name="attn",
    )(qt, k, vt)


def _ssm_kernel(u_ref, bre_ref, bim_ref, cre_ref, cim_ref, d_ref, negr_ref, negi_ref, posr_ref, posi_ref,
                nxtr_ref, nxti_ref, tri_ref, z_ref, cr_ref, ci_ref):
    tc = u_ref.shape[1]

    @pl.when(pl.program_id(1) == 0)
    def _():
        cr_ref[...] = jnp.zeros_like(cr_ref)
        ci_ref[...] = jnp.zeros_like(ci_ref)

    u = u_ref[...]
    ub = u.astype(BF16)
    n_state, n_chan = bre_ref.shape

    def block_diag_dot(w_ref, rhs, blocks):
        rb, cb = w_ref.shape[0] // blocks, w_ref.shape[1] // blocks
        return jnp.concatenate(
            [jnp.dot(w_ref[i * rb:(i + 1) * rb, i * cb:(i + 1) * cb], rhs[i * cb:(i + 1) * cb, :],
                     preferred_element_type=F32) for i in range(blocks)], axis=0)

    in_blocks = n_chan // MXU_TILE
    out_blocks = n_state // MXU_TILE
    bur = block_diag_dot(bre_ref, ub, in_blocks)
    bui = block_diag_dot(bim_ref, ub, in_blocks)
    nr, ni = negr_ref[...], negi_ref[...]
    tri = tri_ref[...]

    def prefix_sum(z):
        hi = z.astype(BF16)
        lo = (z - hi.astype(F32)).astype(BF16)
        return (jnp.dot(hi, tri, preferred_element_type=F32) + jnp.dot(lo, tri, preferred_element_type=F32))

    reps = tc // LANE
    tr = prefix_sum(nr * bur - ni * bui) + jnp.concatenate([cr_ref[...]] * reps, axis=1)
    ti = prefix_sum(nr * bui + ni * bur) + jnp.concatenate([ci_ref[...]] * reps, axis=1)
    pr, pi = posr_ref[...], posi_ref[...]
    xr = pr * tr - pi * ti
    xi = pr * ti + pi * tr
    lr, li = tr[:, tc - LANE:], ti[:, tc - LANE:]
    qr, qi = nxtr_ref[...], nxti_ref[...]
    cr_ref[...] = jnp.broadcast_to((qr * lr - qi * li)[:, LANE - 1:], cr_ref.shape)
    ci_ref[...] = jnp.broadcast_to((qr * li + qi * lr)[:, LANE - 1:], ci_ref.shape)
    y = (block_diag_dot(cre_ref, xr.astype(BF16), out_blocks)
         - block_diag_dot(cim_ref, xi.astype(BF16), out_blocks)
         + d_ref[...] * u)
    z_ref[...] = _gelu(y).T.astype(z_ref.dtype)


def _ssm(ut, bre, bim, cre, cim, d, tabs, tri):
    B, W, L = ut.shape
    tc = SSM_CHUNK
    first = (PREFIX - tc) // tc
    n_real = (L - PREFIX) // tc
    const = lambda shape: pl.BlockSpec(shape, lambda b, i: (0, 0), pipeline_mode=pl.Buffered(1))
    return pl.pallas_call(
        _ssm_kernel,
        grid=(B, 1 + n_real),
        in_specs=[pl.BlockSpec((None, W, tc), lambda b, i: (b, 0, i + first)),
                  const((SSM_LANES, W)), const((SSM_LANES, W)),
                  const((W, SSM_LANES)), const((W, SSM_LANES)), const((W, tc))]
                 + [const((SSM_LANES, tc))] * 4 + [const((SSM_LANES, LANE))] * 2 + [const((tc, tc))],
        out_specs=pl.BlockSpec((None, tc, W), lambda b, i: (b, jnp.maximum(i - 1, 0), 0)),
        out_shape=jax.ShapeDtypeStruct((B, L - PREFIX, W), BF16),
        scratch_shapes=[pltpu.VMEM((SSM_LANES, LANE), F32), pltpu.VMEM((SSM_LANES, LANE), F32)],
        compiler_params=pltpu.CompilerParams(
            dimension_semantics=("arbitrary", "arbitrary"), vmem_limit_bytes=VMEM_LIMIT),
        name="ssm",
    )(ut, bre, bim, cre, cim, d, *tabs, tri)


def _ssm_params(a_re, a_im, log_dt, b_re, b_im, c_re, c_im, d_skip):
    G, P = a_re.shape
    dt = jnp.exp(log_dt.astype(F32))[:, None]
    ar, ai = a_re.astype(F32), a_im.astype(F32)
    mag = jnp.exp(ar * dt)
    abar_r = mag * jnp.cos(ai * dt)
    abar_i = mag * jnp.sin(ai * dt)
    num_r, num_i = abar_r - 1.0, abar_i
    den = ar * ar + ai * ai
    coef_r = (num_r * ar + num_i * ai) / den
    coef_i = (num_i * ar - num_r * ai) / den
    br, bi = b_re.astype(F32), b_im.astype(F32)
    bbar_r = coef_r[..., None] * br - coef_i[..., None] * bi
    bbar_i = coef_r[..., None] * bi + coef_i[..., None] * br
    eye = jnp.eye(G, dtype=F32)
    bd_in = lambda w: jnp.einsum('gpc,gh->gphc', w, eye).reshape(G * P, G * SSM_GROUP)
    bd_out = lambda w: jnp.einsum('gcp,gh->gchp', w.astype(F32), eye).reshape(G * SSM_GROUP, G * P)
    k = jnp.arange(SSM_CHUNK, dtype=F32)[None, :]
    lr = (ar * dt).reshape(G * P, 1)
    li = (ai * dt).reshape(G * P, 1)

    def power(kk):
        m = jnp.exp(kk * lr)
        return m * jnp.cos(kk * li), m * jnp.sin(kk * li)

    neg, pos, nxt = power(-k), power(k), power(k[:, SSM_CHUNK - LANE:] + 1.0)
    d_tab = jnp.broadcast_to(d_skip.astype(F32).reshape(G * SSM_GROUP, 1), (G * SSM_GROUP, SSM_CHUNK))
    tri = jnp.triu(jnp.ones((SSM_CHUNK, SSM_CHUNK), F32)).astype(BF16)
    return (bd_in(bbar_r).astype(BF16), bd_in(bbar_i).astype(BF16),
            bd_out(c_re).astype(BF16), bd_out(c_im).astype(BF16), d_tab, (*neg, *pos, *nxt), tri)


def _merge_kernel(h_ref, att_ref, z_ref, g1_ref, wg_ref, bg_ref, wa_ref, wv_ref, wz_ref, wo_ref, g2_ref,
                  h1_ref, hn2_ref):
    D = h_ref.shape[1]
    h = h_ref[...]
    hn = _rms(h, g1_ref[...]).astype(BF16)
    gate = _sigmoid(jnp.dot(hn, wg_ref[...], preferred_element_type=F32) + bg_ref[...])
    branch_att = jnp.dot(att_ref[...], wa_ref[...], preferred_element_type=F32)
    z = z_ref[...]
    branch_ssm = (jnp.dot(z, wv_ref[...], preferred_element_type=F32)
                  * _sigmoid(jnp.dot(z, wz_ref[...], preferred_element_type=F32)))
    merged = gate[:, :D] * branch_att + gate[:, D:] * branch_ssm
    h1 = h + jnp.dot(merged.astype(BF16), wo_ref[...], preferred_element_type=F32)
    h1_ref[...] = h1
    hn2_ref[...] = _rms(h1, g2_ref[...]).T.astype(hn2_ref.dtype)


def _merge(h, att, z, g1, wg, bg, wa, wv, wz, wo, g2, tm):
    B, S, D = h.shape
    const = lambda shape: pl.BlockSpec(shape, lambda b, i: (0,) * len(shape))
    tok = lambda w: pl.BlockSpec((None, tm, w), lambda b, i: (b, i, 0))
    return pl.pallas_call(
        _merge_kernel,
        grid=(B, S // tm),
        in_specs=[tok(D), tok(ATT_WIDTH), tok(SSM_WIDTH), const((1, D)), const((D, 2 * D)), const((1, 2 * D)),
                  const((ATT_WIDTH, D)), const((SSM_WIDTH, D)), const((SSM_WIDTH, D)),
                  const((D, D)), const((1, D))],
        out_specs=[tok(D), pl.BlockSpec((None, D, tm), lambda b, i: (b, 0, i))],
        out_shape=[jax.ShapeDtypeStruct((B, S, D), F32), jax.ShapeDtypeStruct((B, D, S), BF16)],
        compiler_params=pltpu.CompilerParams(
            dimension_semantics=("arbitrary", "arbitrary"), vmem_limit_bytes=VMEM_LIMIT),
        name="merge",
    )(h, att, z, g1, wg, bg, wa, wv, wz, wo, g2)


def _top16(s, with_rank=False):
    vals = []
    rank = jnp.full(s.shape, float(PEER_TOPK), F32) if with_rank else None
    for a in range(PEER_TOPK):
        m = jnp.max(s, axis=0, keepdims=True)
        vals.append(m)
        hit = s == m
        if with_rank:
            rank = jnp.where(hit, float(a), rank)
        s = jnp.where(hit, -jnp.inf, s)
    return vals, rank


def _peer_kernel(hnt_ref, h1_ref, wq_ref, keys_ref, u_ref, vt_ref, go_ref, o_ref,
                 e1_ref, n1_ref, rank2_ref, e2_ref, acc_ref):
    c = pl.program_id(2)
    tq = hnt_ref.shape[1]
    first_keys = EXPERT_CHUNK // N_KEYS

    @pl.when(c == 0)
    def _():
        acc_ref[...] = jnp.zeros_like(acc_ref)
        qt = jnp.dot(wq_ref[...], hnt_ref[...], preferred_element_type=F32).astype(BF16)
        for h in range(PEER_HEADS):
            r = 2 * h * HALF_KEY
            s1 = jnp.dot(keys_ref[2 * h], qt[r:r + HALF_KEY, :], preferred_element_type=F32)
            s2 = jnp.dot(keys_ref[2 * h + 1], qt[r + HALF_KEY:r + 2 * HALF_KEY, :],
                         preferred_element_type=F32)
            v1, _ = _top16(s1)
            v2, rank2 = _top16(s2, with_rank=True)
            v2_all = jnp.concatenate(v2, axis=0)
            v1_all = jnp.concatenate(v1, axis=0)
            cand = jnp.concatenate(
                [v1[0] + v2_all, v2[0] + v1_all, v2[1] + v1_all[:8]]
                + [v1[a] + v2_all[:8] for a in range(1, 5)], axis=0)
            best, _ = _top16(cand)
            tau = best[-1]
            zsum = sum(jnp.exp(b - best[0]) for b in best)
            n1 = jnp.zeros_like(s1)
            for a in range(PEER_TOPK):
                n_a = jnp.sum(jnp.where(v1[a] + v2_all >= tau, 1.0, 0.0), axis=0, keepdims=True)
                n1 = jnp.where(s1 == v1[a], n_a, n1)
            n1_ref[h] = n1
            rank2_ref[h] = rank2.astype(BF16)
            e1_ref[h] = jnp.exp(s1 - v1[0])
            e2_ref[h] = (jnp.exp(s2 - v2[0]) / zsum).astype(BF16)

    hnt = hnt_ref[...]
    ws = []
    for ii in range(first_keys):
        i = c * first_keys + ii
        act = jnp.dot(u_ref[ii * N_KEYS:(ii + 1) * N_KEYS, :], hnt, preferred_element_type=F32)
        g = jnp.zeros((N_KEYS, tq), BF16)
        for h in range(PEER_HEADS):
            n1 = n1_ref[h, pl.ds(i, 1), :].astype(BF16)
            e1 = e1_ref[h, pl.ds(i, 1), :].astype(BF16)
            g = g + jnp.where(rank2_ref[h] < n1, e2_ref[h] * e1, 0.0)
        ws.append(g * _gelu(act.astype(BF16)))
    acc_ref[...] += jnp.dot(vt_ref[...], jnp.concatenate(ws, axis=0), preferred_element_type=F32)

    @pl.when(c == pl.num_programs(2) - 1)
    def _():
        h2 = h1_ref[...] + acc_ref[...].T
        o_ref[...] = _rms(h2, go_ref[...])


def _peer(hn2t, h1, wq_t, keys, u, v_t, g_out, tq):
    B, S, D = h1.shape
    n_experts = u.shape[0]
    tok = pl.BlockSpec((None, tq, D), lambda b, i, c: (b, i, 0))
    const = lambda shape: pl.BlockSpec(shape, lambda b, i, c: (0,) * len(shape))
    per_head = lambda dtype: pltpu.VMEM((PEER_HEADS, N_KEYS, tq), dtype)
    return pl.pallas_call(
        _peer_kernel,
        grid=(B, S // tq, n_experts // EXPERT_CHUNK),
        in_specs=[pl.BlockSpec((None, D, tq), lambda b, i, c: (b, 0, i)),
                  tok, const(wq_t.shape), const(keys.shape),
                  pl.BlockSpec((EXPERT_CHUNK, D), lambda b, i, c: (c, 0)),
                  pl.BlockSpec((D, EXPERT_CHUNK), lambda b, i, c: (0, c)),
                  const((1, D))],
        out_specs=tok,
        out_shape=jax.ShapeDtypeStruct((B, S, D), F32),
        scratch_shapes=[per_head(F32), per_head(F32), per_head(BF16), per_head(BF16),
                        pltpu.VMEM((D, tq), F32)],
        compiler_params=pltpu.CompilerParams(
            dimension_semantics=("arbitrary", "arbitrary", "arbitrary"), vmem_limit_bytes=VMEM_LIMIT),
        name="peer",
    )(hn2t, h1, wq_t, keys, u, v_t, g_out)


def kernel(x, meta_tokens, mix_norm_g, w_in, b_forget, b_gate, w_att_branch, ssm_a_re, ssm_a_im, ssm_log_dt,
           ssm_b_re, ssm_b_im, ssm_c_re, ssm_c_im, ssm_d, w_glu_val, w_glu_gate, w_out, ffn_norm_g,
           w_query, sub_keys, expert_u, expert_v, out_norm_g):
    B, S, D = x.shape
    tm = TOKEN_TILE
    assert S % tm == 0 and w_in.shape[0] == 1, "one layer over a whole number of row tiles"
    layer = 0
    row = lambda v: v.astype(F32).reshape(1, -1)
    pre = jnp.concatenate([jnp.zeros((PREFIX - N_META, D), x.dtype), meta_tokens.astype(x.dtype)], axis=0)

    wl = w_in[layer]
    o_f = 3 * ATT_WIDTH
    o_u = o_f + ATT_HEADS
    o_g = o_u + SSM_WIDTH
    w1t = jnp.concatenate([wl[:, :ATT_WIDTH] * (ATT_HEAD_DIM ** -0.5), wl[:, 2 * ATT_WIDTH:o_f], wl[:, o_u:o_g]],
                          axis=1).T.astype(BF16)
    w1 = jnp.concatenate([wl[:, ATT_WIDTH:2 * ATT_WIDTH],
                          jnp.pad(wl[:, o_f:o_u], ((0, 0), (0, LANE - ATT_HEADS)))], axis=1).astype(BF16)
    bf = jnp.pad(b_forget[layer].astype(F32), (0, LANE - ATT_HEADS)).reshape(1, LANE)
    qt, k, vt, ut = _inproj(x, pre, row(mix_norm_g[layer]), w1t, w1, bf, tm)
    att = _attn(qt, k, vt, tm)
    z = _ssm(ut, *_ssm_params(
        ssm_a_re[layer], ssm_a_im[layer], ssm_log_dt[layer], ssm_b_re[layer], ssm_b_im[layer],
        ssm_c_re[layer], ssm_c_im[layer], ssm_d[layer]))
    h1, hn2t = _merge(x, att, z, row(mix_norm_g[layer]), wl[:, o_g:].astype(BF16), row(b_gate[layer]),
                      w_att_branch[layer].astype(BF16), w_glu_val[layer].astype(BF16),
                      w_glu_gate[layer].astype(BF16), w_out[layer].astype(BF16), row(ffn_norm_g[layer]), tm)
    keys = sub_keys[layer].astype(BF16).reshape(2 * PEER_HEADS, N_KEYS, HALF_KEY)
    return _peer(hn2t, h1, w_query[layer].T.astype(BF16), keys, expert_u[layer].astype(BF16),
                 expert_v[layer].T.astype(BF16), row(out_norm_g), tm)
```

```python
import jax
import jax.numpy as jnp
import numpy as np
from jax import lax
from jax.experimental import pallas as pl
from jax.experimental.pallas import tpu as pltpu

F32 = jnp.float32
BF16 = jnp.bfloat16

N_META = 16
ATT_HEADS = 8
ATT_HEAD_DIM = 64
ATT_WIDTH = ATT_HEADS * ATT_HEAD_DIM
SSM_WIDTH = 512
SSM_GROUP = 16
SSM_GROUPS = SSM_WIDTH // SSM_GROUP
SSM_STATE = 64
SSM_LANES = SSM_GROUPS * SSM_STATE
PEER_HEADS = 8
N_KEYS = 128
PEER_TOPK = 16
HALF_KEY = 128
RMS_EPS = 1e-6
NEG_INF = -1e30

LANE = 128
MXU_TILE = 256
TOKEN_TILE = 512
PREFIX = TOKEN_TILE
META_TILE = 128
HEADS_PER_STEP = 2
KEY_TILES_PER_BODY = 2
SSM_CHUNK = 256
EXPERT_CHUNK = 2048
VMEM_LIMIT = 56 * 1024 * 1024


def _rms(x, g):
    return x * lax.rsqrt(jnp.mean(x * x, axis=-1, keepdims=True) + RMS_EPS) * g


def _gelu(x):
    return 0.5 * x * (1.0 + lax.erf(x * (2.0 ** -0.5)))


def _sigmoid(x):
    return 1.0 / (1.0 + jnp.exp(-x))


def _log_sigmoid(x):
    return jnp.minimum(x, 0.0) - jnp.log1p(jnp.exp(-jnp.abs(x)))


def _split3(c):
    hi = c.astype(BF16).astype(F32)
    r = c - hi
    mid = r.astype(BF16).astype(F32)
    lo = (r - mid).astype(BF16).astype(F32)
    return hi, mid, lo


def _inproj_kernel(x_ref, pre_ref, g_ref, wt_ref, w_ref, bf_ref, q_ref, k_ref, v_ref, u_ref, carry_ref):
    tm = x_ref.shape[0]
    i = pl.program_id(1)

    @pl.when(i == 0)
    def _():
        carry_ref[...] = jnp.zeros_like(carry_ref)

    h = jnp.where(i == 0, pre_ref[...], x_ref[...])
    hn = _rms(h, g_ref[...]).astype(BF16)
    proj_t = lax.dot_general(wt_ref[...], hn, (((1,), (1,)), ((), ())), preferred_element_type=F32)
    proj = jnp.dot(hn, w_ref[...], preferred_element_type=F32)
    u_ref[...] = proj_t[2 * ATT_WIDTH:, :]

    log_f = _log_sigmoid(proj[:, ATT_WIDTH:] + bf_ref[...])
    row = lax.broadcasted_iota(jnp.int32, (tm, tm), 0)
    col = lax.broadcasted_iota(jnp.int32, (tm, tm), 1)
    tri = jnp.where(row >= col, 1.0, 0.0).astype(BF16)
    c = carry_ref[...] + sum(jnp.dot(tri, piece.astype(BF16), preferred_element_type=F32)
                             for piece in _split3(log_f))
    carry_ref[...] = c[tm - 1:tm, :]
    hi, mid, lo = _split3(c)
    hi_t, mid_t, lo_t = _split3(c.T)

    lane = lax.broadcasted_iota(jnp.int32, (tm, ATT_HEAD_DIM), 1)
    sub = lax.broadcasted_iota(jnp.int32, (ATT_HEAD_DIM, tm), 0)
    ones3 = jnp.where(lane < 3, 1.0, 0.0)
    ones3_t = jnp.where((sub >= 3) & (sub < 6), 1.0, 0.0)
    v_ext_t = jnp.where(sub == 0, 1.0, 0.0)
    for hd in range(ATT_HEADS):
        lo_c, hi_c = hd * ATT_HEAD_DIM, (hd + 1) * ATT_HEAD_DIM
        q_ext_t = ones3_t + jnp.where(sub == 0, hi_t[hd:hd + 1], jnp.where(
            sub == 1, mid_t[hd:hd + 1], jnp.where(sub == 2, lo_t[hd:hd + 1], 0.0)))
        q_ref[hd] = jnp.concatenate([proj_t[lo_c:hi_c, :], q_ext_t], axis=0).astype(BF16)
        v_ref[hd] = jnp.concatenate([proj_t[ATT_WIDTH + lo_c:ATT_WIDTH + hi_c, :], v_ext_t], axis=0).astype(BF16)
        ch, cm, cl = hi[:, hd:hd + 1], mid[:, hd:hd + 1], lo[:, hd:hd + 1]
        k_ext = ones3 - jnp.where(lane == 3, ch, jnp.where(lane == 4, cm, jnp.where(lane == 5, cl, 0.0)))
        k_ref[hd] = jnp.concatenate([proj[:, lo_c:hi_c].astype(BF16), k_ext.astype(BF16)], axis=-1)


def _inproj(x, pre, g, wt, w, bf, tm):
    B, S, D = x.shape
    L = PREFIX + S
    ncol = w.shape[1]
    head_shape = jax.ShapeDtypeStruct((B, ATT_HEADS, L, LANE), BF16)
    head_spec = pl.BlockSpec((None, ATT_HEADS, tm, LANE), lambda b, i: (b, 0, i, 0))
    head_t_shape = jax.ShapeDtypeStruct((B, ATT_HEADS, LANE, L), BF16)
    head_t_spec = pl.BlockSpec((None, ATT_HEADS, LANE, tm), lambda b, i: (b, 0, 0, i))
    const = lambda shape: pl.BlockSpec(shape, lambda b, i: (0, 0))
    return pl.pallas_call(
        _inproj_kernel,
        grid=(B, L // tm),
        in_specs=[pl.BlockSpec((None, tm, D), lambda b, i: (b, jnp.maximum(i - 1, 0), 0)),
                  const((tm, D)), const((1, D)), const(wt.shape), const((D, ncol)), const((1, LANE))],
        out_specs=[head_t_spec, head_spec, head_t_spec,
                   pl.BlockSpec((None, SSM_WIDTH, tm), lambda b, i: (b, 0, i))],
        out_shape=[head_t_shape, head_shape, head_t_shape,
                   jax.ShapeDtypeStruct((B, SSM_WIDTH, L), F32)],
        scratch_shapes=[pltpu.VMEM((1, LANE), F32)],
        compiler_params=pltpu.CompilerParams(
            dimension_semantics=("arbitrary", "arbitrary"), vmem_limit_bytes=VMEM_LIMIT),
        name="inproj",
    )(x, pre, g, wt, w, bf)


def _attn_kernel(q_ref, k_ref, v_ref, o_ref, m_ref, acc_ref, alpha_ref, p_ref):
    qi = pl.program_id(2)
    tq = q_ref.shape[2]
    tk = tq
    heads = range(HEADS_PER_STEP)

    def scores(hh, start, size):
        return jnp.dot(k_ref[hh, pl.ds(start, size), :], q_ref[hh], preferred_element_type=F32)

    def tile_start(j):
        return pl.multiple_of(PREFIX + j * tk, tk)

    def softmax_tile(hh, s):
        m_prev = m_ref[hh]
        m_new = jnp.maximum(m_prev, jnp.max(s, axis=0, keepdims=True))
        m_ref[hh] = m_new
        alpha_ref[hh] = jnp.exp(m_prev - m_new)
        p_ref[hh] = jnp.exp(s - m_new).astype(BF16)

    def flush(hh, j):
        acc = alpha_ref[hh] * acc_ref[hh] + jnp.dot(v_ref[hh, :, pl.ds(tile_start(j), tk)], p_ref[hh],
                                                    preferred_element_type=F32)
        acc_ref[hh] = acc
        return acc

    key = lax.broadcasted_iota(jnp.int32, (META_TILE, tq), 0)
    for hh in heads:
        s = jnp.where(key >= META_TILE - N_META, scores(hh, PREFIX - META_TILE, META_TILE), NEG_INF)
        m0 = jnp.max(s, axis=0, keepdims=True)
        p0 = jnp.exp(s - m0).astype(BF16)
        acc_ref[hh] = jnp.dot(v_ref[hh, :, PREFIX - META_TILE:PREFIX], p0, preferred_element_type=F32)
        m_ref[hh] = m0

    key = lax.broadcasted_iota(jnp.int32, (tk, tq), 0)
    qry = lax.broadcasted_iota(jnp.int32, (tk, tq), 1)
    for hh in heads:
        softmax_tile(hh, jnp.where(key <= qry, scores(hh, tile_start(qi), tk), NEG_INF))

    def one_tile(j, pending):
        for hh in heads:
            flush(hh, pending)
            softmax_tile(hh, scores(hh, tile_start(j), tk))
        return j

    def unrolled(jj, pending):
        for t in range(KEY_TILES_PER_BODY):
            pending = one_tile(KEY_TILES_PER_BODY * jj + t, pending)
        return pending

    rest = qi % KEY_TILES_PER_BODY
    pending = lax.fori_loop(0, qi // KEY_TILES_PER_BODY, unrolled, qi)
    pending = lax.fori_loop(0, rest, lambda r, pend: one_tile(qi - rest + r, pend), pending)

    outs = []
    for hh in heads:
        acc = flush(hh, pending).T
        outs.append(acc[:, :ATT_HEAD_DIM] / acc[:, ATT_HEAD_DIM:ATT_HEAD_DIM + 1])
    o_ref[...] = jnp.concatenate(outs, axis=-1).astype(o_ref.dtype)


def _attn(qt, k, vt, tq):
    B, H, L, _ = k.shape
    first = PREFIX // tq
    hp = HEADS_PER_STEP
    return pl.pallas_call(
        _attn_kernel,
        grid=(B, H // hp, (L - PREFIX) // tq),
        in_specs=[pl.BlockSpec((None, hp, LANE, tq), lambda b, h, i: (b, h, 0, i + first)),
                  pl.BlockSpec((None, hp, L, LANE), lambda b, h, i: (b, h, 0, 0)),
                  pl.BlockSpec((None, hp, LANE, L), lambda b, h, i: (b, h, 0, 0))],
        out_specs=pl.BlockSpec((None, tq, hp * ATT_HEAD_DIM), lambda b, h, i: (b, i, h)),
        out_shape=jax.ShapeDtypeStruct((B, L - PREFIX, H * ATT_HEAD_DIM), BF16),
        scratch_shapes=[pltpu.VMEM((hp, 1, tq), F32), pltpu.VMEM((hp, LANE, tq), F32),
                        pltpu.VMEM((hp, 1, tq), F32), pltpu.VMEM((hp, tq, tq), BF16)],
        compiler_params=pltpu.CompilerParams(
            dimension_semantics=("arbitrary", "arbitrary", "arbitrary"), vmem_limit_bytes=VMEM_LIMIT),
        name="attn",
    )(qt, k, vt)


def _ssm_kernel(u_ref, bre_ref, bim_ref, cre_ref, cim_ref, d_ref, negr_ref, negi_ref, posr_ref, posi_ref,
                nxtr_ref, nxti_ref, tri_ref, z_ref, cr_ref, ci_ref):
    tc = u_ref.shape[1]

    @pl.when(pl.program_id(1) == 0)
    def _():
        cr_ref[...] = jnp.zeros_like(cr_ref)
        ci_ref[...] = jnp.zeros_like(ci_ref)

    u = u_ref[...]
    ub = u.astype(BF16)
    n_state, n_chan = bre_ref.shape

    def block_diag_dot(w_ref, rhs, blocks):
        rb, cb = w_ref.shape[0] // blocks, w_ref.shape[1] // blocks
        return jnp.concatenate(
            [jnp.dot(w_ref[i * rb:(i + 1) * rb, i * cb:(i + 1) * cb], rhs[i * cb:(i + 1) * cb, :],
                     preferred_element_type=F32) for i in range(blocks)], axis=0)

    in_blocks = n_chan // MXU_TILE
    out_blocks = n_state // MXU_TILE
    bur = block_diag_dot(bre_ref, ub, in_blocks)
    bui = block_diag_dot(bim_ref, ub, in_blocks)
    nr, ni = negr_ref[...], negi_ref[...]
    tri = tri_ref[...]

    def prefix_sum(z):
        hi = z.astype(BF16)
        lo = (z - hi.astype(F32)).astype(BF16)
        return (jnp.dot(hi, tri, preferred_element_type=F32) + jnp.dot(lo, tri, preferred_element_type=F32))

    reps = tc // LANE
    tr = prefix_sum(nr * bur - ni * bui) + jnp.concatenate([cr_ref[...]] * reps, axis=1)
    ti = prefix_sum(nr * bui + ni * bur) + jnp.concatenate([ci_ref[...]] * reps, axis=1)
    pr, pi = posr_ref[...], posi_ref[...]
    xr = pr * tr - pi * ti
    xi = pr * ti + pi * tr
    lr, li = tr[:, tc - LANE:], ti[:, tc - LANE:]
    qr, qi = nxtr_ref[...], nxti_ref[...]
    cr_ref[...] = jnp.broadcast_to((qr * lr - qi * li)[:, LANE - 1:], cr_ref.shape)
    ci_ref[...] = jnp.broadcast_to((qr * li + qi * lr)[:, LANE - 1:], ci_ref.shape)
    y = (block_diag_dot(cre_ref, xr.astype(BF16), out_blocks)
         - block_diag_dot(cim_ref, xi.astype(BF16), out_blocks)
         + d_ref[...] * u)
    z_ref[...] = _gelu(y).T.astype(z_ref.dtype)


def _ssm(ut, bre, bim, cre, cim, d, tabs, tri):
    B, W, L = ut.shape
    tc = SSM_CHUNK
    first = (PREFIX - tc) // tc
    n_real = (L - PREFIX) // tc
    const = lambda shape: pl.BlockSpec(shape, lambda b, i: (0, 0), pipeline_mode=pl.Buffered(1))
    return pl.pallas_call(
        _ssm_kernel,
        grid=(B, 1 + n_real),
        in_specs=[pl.BlockSpec((None, W, tc), lambda b, i: (b, 0, i + first)),
                  const((SSM_LANES, W)), const((SSM_LANES, W)),
                  const((W, SSM_LANES)), const((W, SSM_LANES)), const((W, tc))]
                 + [const((SSM_LANES, tc))] * 4 + [const((SSM_LANES, LANE))] * 2 + [const((tc, tc))],
        out_specs=pl.BlockSpec((None, tc, W), lambda b, i: (b, jnp.maximum(i - 1, 0), 0)),
        out_shape=jax.ShapeDtypeStruct((B, L - PREFIX, W), BF16),
        scratch_shapes=[pltpu.VMEM((SSM_LANES, LANE), F32), pltpu.VMEM((SSM_LANES, LANE), F32)],
        compiler_params=pltpu.CompilerParams(
            dimension_semantics=("arbitrary", "arbitrary"), vmem_limit_bytes=VMEM_LIMIT),
        name="ssm",
    )(ut, bre, bim, cre, cim, d, *tabs, tri)


def _ssm_params(a_re, a_im, log_dt, b_re, b_im, c_re, c_im, d_skip):
    G, P = a_re.shape
    dt = jnp.exp(log_dt.astype(F32))[:, None]
    ar, ai = a_re.astype(F32), a_im.astype(F32)
    mag = jnp.exp(ar * dt)
    abar_r = mag * jnp.cos(ai * dt)
    abar_i = mag * jnp.sin(ai * dt)
    num_r, num_i = abar_r - 1.0, abar_i
    den = ar * ar + ai * ai
    coef_r = (num_r * ar + num_i * ai) / den
    coef_i = (num_i * ar - num_r * ai) / den
    br, bi = b_re.astype(F32), b_im.astype(F32)
    bbar_r = coef_r[..., None] * br - coef_i[..., None] * bi
    bbar_i = coef_r[..., None] * bi + coef_i[..., None] * br
    eye = jnp.eye(G, dtype=F32)
    bd_in = lambda w: jnp.einsum('gpc,gh->gphc', w, eye).reshape(G * P, G * SSM_GROUP)
    bd_out = lambda w: jnp.einsum('gcp,gh->gchp', w.astype(F32), eye).reshape(G * SSM_GROUP, G * P)
    k = jnp.arange(SSM_CHUNK, dtype=F32)[None, :]
    lr = (ar * dt).reshape(G * P, 1)
    li = (ai * dt).reshape(G * P, 1)

    def power(kk):
        m = jnp.exp(kk * lr)
        return m * jnp.cos(kk * li), m * jnp.sin(kk * li)

    mid = SSM_CHUNK // 2
    neg, pos, nxt = power(mid - k), power(k - mid), power(k[:, SSM_CHUNK - LANE:] + 1.0)
    d_tab = jnp.broadcast_to(d_skip.astype(F32).reshape(G * SSM_GROUP, 1), (G * SSM_GROUP, SSM_CHUNK))
    tri = jnp.triu(jnp.ones((SSM_CHUNK, SSM_CHUNK), F32)).astype(BF16)
    return (bd_in(bbar_r).astype(BF16), bd_in(bbar_i).astype(BF16),
            bd_out(c_re).astype(BF16), bd_out(c_im).astype(BF16), d_tab, (*neg, *pos, *nxt), tri)


def _merge_kernel(h_ref, att_ref, z_ref, g1_ref, wg_ref, bg_ref, wa_ref, wv_ref, wz_ref, wo_ref, g2_ref,
                  h1_ref, hn2_ref):
    D = h_ref.shape[1]
    h = h_ref[...]
    hn = _rms(h, g1_ref[...]).astype(BF16)
    gate = _sigmoid(jnp.dot(hn, wg_ref[...], preferred_element_type=F32) + bg_ref[...])
    branch_att = jnp.dot(att_ref[...], wa_ref[...], preferred_element_type=F32)
    z = z_ref[...]
    branch_ssm = (jnp.dot(z, wv_ref[...], preferred_element_type=F32)
                  * _sigmoid(jnp.dot(z, wz_ref[...], preferred_element_type=F32)))
    merged = gate[:, :D] * branch_att + gate[:, D:] * branch_ssm
    h1 = h + jnp.dot(merged.astype(BF16), wo_ref[...], preferred_element_type=F32)
    h1_ref[...] = h1
    hn2_ref[...] = _rms(h1, g2_ref[...]).T.astype(hn2_ref.dtype)


def _merge(h, att, z, g1, wg, bg, wa, wv, wz, wo, g2, tm):
    B, S, D = h.shape
    const = lambda shape: pl.BlockSpec(shape, lambda b, i: (0,) * len(shape))
    tok = lambda w: pl.BlockSpec((None, tm, w), lambda b, i: (b, i, 0))
    return pl.pallas_call(
        _merge_kernel,
        grid=(B, S // tm),
        in_specs=[tok(D), tok(ATT_WIDTH), tok(SSM_WIDTH), const((1, D)), const((D, 2 * D)), const((1, 2 * D)),
                  const((ATT_WIDTH, D)), const((SSM_WIDTH, D)), const((SSM_WIDTH, D)),
                  const((D, D)), const((1, D))],
        out_specs=[tok(D), pl.BlockSpec((None, D, tm), lambda b, i: (b, 0, i))],
        out_shape=[jax.ShapeDtypeStruct((B, S, D), F32), jax.ShapeDtypeStruct((B, D, S), BF16)],
        compiler_params=pltpu.CompilerParams(
            dimension_semantics=("arbitrary", "arbitrary"), vmem_limit_bytes=VMEM_LIMIT),
        name="merge",
    )(h, att, z, g1, wg, bg, wa, wv, wz, wo, g2)


def _top16(s, with_rank=False):
    vals = []
    rank = jnp.full(s.shape, float(PEER_TOPK), F32) if with_rank else None
    for a in range(PEER_TOPK):
        m = jnp.max(s, axis=0, keepdims=True)
        vals.append(m)
        hit = s == m
        if with_rank:
            rank = jnp.where(hit, float(a), rank)
        s = jnp.where(hit, -jnp.inf, s)
    return vals, rank


def _peer_kernel(hnt_ref, h1_ref, wq_ref, keys_ref, u_ref, vt_ref, go_ref, o_ref,
                 e1_ref, n1_ref, rank2_ref, e2_ref, acc_ref):
    c = pl.program_id(2)
    tq = hnt_ref.shape[1]
    first_keys = EXPERT_CHUNK // N_KEYS

    @pl.when(c == 0)
    def _():
        acc_ref[...] = jnp.zeros_like(acc_ref)
        qt = jnp.dot(wq_ref[...], hnt_ref[...], preferred_element_type=F32).astype(BF16)
        for h in range(PEER_HEADS):
            r = 2 * h * HALF_KEY
            s1 = jnp.dot(keys_ref[2 * h], qt[r:r + HALF_KEY, :], preferred_element_type=F32)
            s2 = jnp.dot(keys_ref[2 * h + 1], qt[r + HALF_KEY:r + 2 * HALF_KEY, :],
                         preferred_element_type=F32)
            v1, _ = _top16(s1)
            v2, rank2 = _top16(s2, with_rank=True)
            v2_all = jnp.concatenate(v2, axis=0)
            v1_all = jnp.concatenate(v1, axis=0)
            cand = jnp.concatenate(
                [v1[0] + v2_all, v2[0] + v1_all, v2[1] + v1_all[:8]]
                + [v1[a] + v2_all[:8] for a in range(1, 5)], axis=0)
            best, _ = _top16(cand)
            tau = best[-1]
            zsum = sum(jnp.exp(b - best[0]) for b in best)
            n1 = jnp.zeros_like(s1)
            for a in range(PEER_TOPK):
                n_a = jnp.sum(jnp.where(v1[a] + v2_all >= tau, 1.0, 0.0), axis=0, keepdims=True)
                n1 = jnp.where(s1 == v1[a], n_a, n1)
            n1_ref[h] = n1
            rank2_ref[h] = rank2.astype(BF16)
            e1_ref[h] = jnp.exp(s1 - v1[0])
            e2_ref[h] = (jnp.exp(s2 - v2[0]) / zsum).astype(BF16)

    hnt = hnt_ref[...]
    ws = []
    for ii in range(first_keys):
        i = c * first_keys + ii
        act = jnp.dot(u_ref[ii * N_KEYS:(ii + 1) * N_KEYS, :], hnt, preferred_element_type=F32)
        g = jnp.zeros((N_KEYS, tq), BF16)
        for h in range(PEER_HEADS):
            n1 = n1_ref[h, pl.ds(i, 1), :].astype(BF16)
            e1 = e1_ref[h, pl.ds(i, 1), :].astype(BF16)
            g = g + jnp.where(rank2_ref[h] < n1, e2_ref[h] * e1, 0.0)
        ws.append(g * _gelu(act.astype(BF16)))
    acc_ref[...] += jnp.dot(vt_ref[...], jnp.concatenate(ws, axis=0), preferred_element_type=F32)

    @pl.when(c == pl.num_programs(2) - 1)
    def _():
        h2 = h1_ref[...] + acc_ref[...].T
        o_ref[...] = _rms(h2, go_ref[...])


def _peer(hn2t, h1, wq_t, keys, u, v_t, g_out, tq):
    B, S, D = h1.shape
    n_experts = u.shape[0]
    tok = pl.BlockSpec((None, tq, D), lambda b, i, c: (b, i, 0))
    const = lambda shape: pl.BlockSpec(shape, lambda b, i, c: (0,) * len(shape))
    per_head = lambda dtype: pltpu.VMEM((PEER_HEADS, N_KEYS, tq), dtype)
    return pl.pallas_call(
        _peer_kernel,
        grid=(B, S // tq, n_experts // EXPERT_CHUNK),
        in_specs=[pl.BlockSpec((None, D, tq), lambda b, i, c: (b, 0, i)),
                  tok, const(wq_t.shape), const(keys.shape),
                  pl.BlockSpec((EXPERT_CHUNK, D), lambda b, i, c: (c, 0)),
                  pl.BlockSpec((D, EXPERT_CHUNK), lambda b, i, c: (0, c)),
                  const((1, D))],
        out_specs=tok,
        out_shape=jax.ShapeDtypeStruct((B, S, D), F32),
        scratch_shapes=[per_head(F32), per_head(F32), per_head(BF16), per_head(BF16),
                        pltpu.VMEM((D, tq), F32)],
        compiler_params=pltpu.CompilerParams(
            dimension_semantics=("arbitrary", "arbitrary", "arbitrary"), vmem_limit_bytes=VMEM_LIMIT),
        name="peer",
    )(hn2t, h1, wq_t, keys, u, v_t, g_out)


def kernel(x, meta_tokens, mix_norm_g, w_in, b_forget, b_gate, w_att_branch, ssm_a_re, ssm_a_im, ssm_log_dt,
           ssm_b_re, ssm_b_im, ssm_c_re, ssm_c_im, ssm_d, w_glu_val, w_glu_gate, w_out, ffn_norm_g,
           w_query, sub_keys, expert_u, expert_v, out_norm_g):
    B, S, D = x.shape
    tm = TOKEN_TILE
    assert S % tm == 0 and w_in.shape[0] == 1, "one layer over a whole number of row tiles"
    layer = 0
    row = lambda v: v.astype(F32).reshape(1, -1)
    pre = jnp.concatenate([jnp.zeros((PREFIX - N_META, D), x.dtype), meta_tokens.astype(x.dtype)], axis=0)

    wl = w_in[layer]
    o_f = 3 * ATT_WIDTH
    o_u = o_f + ATT_HEADS
    o_g = o_u + SSM_WIDTH
    w1t = jnp.concatenate([wl[:, :ATT_WIDTH] * (ATT_HEAD_DIM ** -0.5), wl[:, 2 * ATT_WIDTH:o_f], wl[:, o_u:o_g]],
                          axis=1).T.astype(BF16)
    w1 = jnp.concatenate([wl[:, ATT_WIDTH:2 * ATT_WIDTH],
                          jnp.pad(wl[:, o_f:o_u], ((0, 0), (0, LANE - ATT_HEADS)))], axis=1).astype(BF16)
    bf = jnp.pad(b_forget[layer].astype(F32), (0, LANE - ATT_HEADS)).reshape(1, LANE)
    qt, k, vt, ut = _inproj(x, pre, row(mix_norm_g[layer]), w1t, w1, bf, tm)
    att = _attn(qt, k, vt, tm)
    z = _ssm(ut, *_ssm_params(
        ssm_a_re[layer], ssm_a_im[layer], ssm_log_dt[layer], ssm_b_re[layer], ssm_b_im[layer],
        ssm_c_re[layer], ssm_c_im[layer], ssm_d[layer]))
    h1, hn2t = _merge(x, att, z, row(mix_norm_g[layer]), wl[:, o_g:].astype(BF16), row(b_gate[layer]),
                      w_att_branch[layer].astype(BF16), w_glu_val[layer].astype(BF16),
                      w_glu_gate[layer].astype(BF16), w_out[layer].astype(BF16), row(ffn_norm_g[layer]), tm)
    keys = sub_keys[layer].astype(BF16).reshape(2 * PEER_HEADS, N_KEYS, HALF_KEY)
    return _peer(hn2t, h1, w_query[layer].T.astype(BF16), keys, expert_u[layer].astype(BF16),
                 expert_v[layer].T.astype(BF16), row(out_norm_g), tm)
```

```python
import jax
import jax.numpy as jnp
import numpy as np
from jax import lax
from jax.experimental import pallas as pl
from jax.experimental.pallas import tpu as pltpu

F32 = jnp.float32
BF16 = jnp.bfloat16

N_META = 16
ATT_HEADS = 8
ATT_HEAD_DIM = 64
ATT_WIDTH = ATT_HEADS * ATT_HEAD_DIM
SSM_WIDTH = 512
SSM_GROUP = 16
SSM_GROUPS = SSM_WIDTH // SSM_GROUP
SSM_STATE = 64
SSM_LANES = SSM_GROUPS * SSM_STATE
PEER_HEADS = 8
N_KEYS = 128
PEER_TOPK = 16
HALF_KEY = 128
RMS_EPS = 1e-6
NEG_INF = -1e30

LANE = 128
MXU_TILE = 256
TOKEN_TILE = 512
PREFIX = TOKEN_TILE
META_TILE = 128
HEADS_PER_STEP = 2
KEY_TILES_PER_BODY = 2
SSM_CHUNK = 256
EXPERT_CHUNK = 2048
VMEM_LIMIT = 56 * 1024 * 1024


def _rms(x, g):
    return x * lax.rsqrt(jnp.mean(x * x, axis=-1, keepdims=True) + RMS_EPS) * g


def _gelu(x):
    return 0.5 * x * (1.0 + lax.erf(x * (2.0 ** -0.5)))


def _sigmoid(x):
    return 1.0 / (1.0 + jnp.exp(-x))


def _log_sigmoid(x):
    return jnp.minimum(x, 0.0) - jnp.log1p(jnp.exp(-jnp.abs(x)))


def _split3(c):
    hi = c.astype(BF16).astype(F32)
    r = c - hi
    mid = r.astype(BF16).astype(F32)
    lo = (r - mid).astype(BF16).astype(F32)
    return hi, mid, lo


def _inproj_kernel(x_ref, pre_ref, g_ref, wt_ref, w_ref, bf_ref, q_ref, k_ref, v_ref, u_ref, carry_ref):
    tm = x_ref.shape[0]
    i = pl.program_id(1)

    @pl.when(i == 0)
    def _():
        carry_ref[...] = jnp.zeros_like(carry_ref)

    h = jnp.where(i == 0, pre_ref[...], x_ref[...])
    hn = _rms(h, g_ref[...]).astype(BF16)
    proj_t = lax.dot_general(wt_ref[...], hn, (((1,), (1,)), ((), ())), preferred_element_type=F32)
    proj = jnp.dot(hn, w_ref[...], preferred_element_type=F32)
    u_ref[...] = proj_t[2 * ATT_WIDTH:, :]

    log_f = _log_sigmoid(proj[:, ATT_WIDTH:] + bf_ref[...])
    row = lax.broadcasted_iota(jnp.int32, (tm, tm), 0)
    col = lax.broadcasted_iota(jnp.int32, (tm, tm), 1)
    tri = jnp.where(row >= col, 1.0, 0.0).astype(BF16)
    c = carry_ref[...] + sum(jnp.dot(tri, piece.astype(BF16), preferred_element_type=F32)
                             for piece in _split3(log_f))
    carry_ref[...] = c[tm - 1:tm, :]
    hi, mid, lo = _split3(c)
    hi_t, mid_t, lo_t = _split3(c.T)

    lane = lax.broadcasted_iota(jnp.int32, (tm, ATT_HEAD_DIM), 1)
    sub = lax.broadcasted_iota(jnp.int32, (ATT_HEAD_DIM, tm), 0)
    ones3 = jnp.where(lane < 3, 1.0, 0.0)
    ones3_t = jnp.where((sub >= 3) & (sub < 6), 1.0, 0.0)
    v_ext_t = jnp.where(sub == 0, 1.0, 0.0)
    for hd in range(ATT_HEADS):
        lo_c, hi_c = hd * ATT_HEAD_DIM, (hd + 1) * ATT_HEAD_DIM
        q_ext_t = ones3_t + jnp.where(sub == 0, hi_t[hd:hd + 1], jnp.where(
            sub == 1, mid_t[hd:hd + 1], jnp.where(sub == 2, lo_t[hd:hd + 1], 0.0)))
        q_ref[hd] = jnp.concatenate([proj_t[lo_c:hi_c, :], q_ext_t], axis=0).astype(BF16)
        v_ref[hd] = jnp.concatenate([proj_t[ATT_WIDTH + lo_c:ATT_WIDTH + hi_c, :], v_ext_t], axis=0).astype(BF16)
        ch, cm, cl = hi[:, hd:hd + 1], mid[:, hd:hd + 1], lo[:, hd:hd + 1]
        k_ext = ones3 - jnp.where(lane == 3, ch, jnp.where(lane == 4, cm, jnp.where(lane == 5, cl, 0.0)))
        k_ref[hd] = jnp.concatenate([proj[:, lo_c:hi_c].astype(BF16), k_ext.astype(BF16)], axis=-1)


def _inproj(x, pre, g, wt, w, bf, tm):
    B, S, D = x.shape
    L = PREFIX + S
    ncol = w.shape[1]
    head_shape = jax.ShapeDtypeStruct((B, ATT_HEADS, L, LANE), BF16)
    head_spec = pl.BlockSpec((None, ATT_HEADS, tm, LANE), lambda b, i: (b, 0, i, 0))
    head_t_shape = jax.ShapeDtypeStruct((B, ATT_HEADS, LANE, L), BF16)
    head_t_spec = pl.BlockSpec((None, ATT_HEADS, LANE, tm), lambda b, i: (b, 0, 0, i))
    const = lambda shape: pl.BlockSpec(shape, lambda b, i: (0, 0))
    return pl.pallas_call(
        _inproj_kernel,
        grid=(B, L // tm),
        in_specs=[pl.BlockSpec((None, tm, D), lambda b, i: (b, jnp.maximum(i - 1, 0), 0)),
                  const((tm, D)), const((1, D)), const(wt.shape), const((D, ncol)), const((1, LANE))],
        out_specs=[head_t_spec, head_spec, head_t_spec,
                   pl.BlockSpec((None, SSM_WIDTH, tm), lambda b, i: (b, 0, i))],
        out_shape=[head_t_shape, head_shape, head_t_shape,
                   jax.ShapeDtypeStruct((B, SSM_WIDTH, L), F32)],
        scratch_shapes=[pltpu.VMEM((1, LANE), F32)],
        compiler_params=pltpu.CompilerParams(
            dimension_semantics=("arbitrary", "arbitrary"), vmem_limit_bytes=VMEM_LIMIT),
        name="inproj",
    )(x, pre, g, wt, w, bf)


def _attn_kernel(q_ref, k_ref, v_ref, o_ref, m_ref, acc_ref, alpha_ref, p_ref):
    qi = pl.program_id(2)
    tq = q_ref.shape[2]
    tk = tq
    heads = range(HEADS_PER_STEP)

    def scores(hh, start, size):
        return jnp.dot(k_ref[hh, pl.ds(start, size), :], q_ref[hh], preferred_element_type=F32)

    def tile_start(j):
        return pl.multiple_of(PREFIX + j * tk, tk)

    def softmax_tile(hh, s):
        m_prev = m_ref[hh]
        m_new = jnp.maximum(m_prev, jnp.max(s, axis=0, keepdims=True))
        m_ref[hh] = m_new
        alpha_ref[hh] = jnp.exp(m_prev - m_new)
        p_ref[hh] = jnp.exp(s - m_new).astype(BF16)

    def flush(hh, j):
        acc = alpha_ref[hh] * acc_ref[hh] + jnp.dot(v_ref[hh, :, pl.ds(tile_start(j), tk)], p_ref[hh],
                                                    preferred_element_type=F32)
        acc_ref[hh] = acc
        return acc

    key = lax.broadcasted_iota(jnp.int32, (META_TILE, tq), 0)
    for hh in heads:
        s = jnp.where(key >= META_TILE - N_META, scores(hh, PREFIX - META_TILE, META_TILE), NEG_INF)
        m0 = jnp.max(s, axis=0, keepdims=True)
        p0 = jnp.exp(s - m0).astype(BF16)
        acc_ref[hh] = jnp.dot(v_ref[hh, :, PREFIX - META_TILE:PREFIX], p0, preferred_element_type=F32)
        m_ref[hh] = m0

    key = lax.broadcasted_iota(jnp.int32, (tk, tq), 0)
    qry = lax.broadcasted_iota(jnp.int32, (tk, tq), 1)
    for hh in heads:
        softmax_tile(hh, jnp.where(key <= qry, scores(hh, tile_start(qi), tk), NEG_INF))

    def one_tile(j, pending):
        for hh in heads:
            flush(hh, pending)
            softmax_tile(hh, scores(hh, tile_start(j), tk))
        return j

    def unrolled(jj, pending):
        for t in range(KEY_TILES_PER_BODY):
            pending = one_tile(KEY_TILES_PER_BODY * jj + t, pending)
        return pending

    rest = qi % KEY_TILES_PER_BODY
    pending = lax.fori_loop(0, qi // KEY_TILES_PER_BODY, unrolled, qi)
    pending = lax.fori_loop(0, rest, lambda r, pend: one_tile(qi - rest + r, pend), pending)

    outs = []
    for hh in heads:
        acc = flush(hh, pending)
        outs.append(acc[:ATT_HEAD_DIM, :] / acc[ATT_HEAD_DIM:ATT_HEAD_DIM + 1, :])
    o_ref[...] = jnp.concatenate(outs, axis=0).astype(o_ref.dtype)


def _attn(qt, k, vt, tq):
    B, H, L, _ = k.shape
    first = PREFIX // tq
    hp = HEADS_PER_STEP
    return pl.pallas_call(
        _attn_kernel,
        grid=(B, H // hp, (L - PREFIX) // tq),
        in_specs=[pl.BlockSpec((None, hp, LANE, tq), lambda b, h, i: (b, h, 0, i + first)),
                  pl.BlockSpec((None, hp, L, LANE), lambda b, h, i: (b, h, 0, 0)),
                  pl.BlockSpec((None, hp, LANE, L), lambda b, h, i: (b, h, 0, 0))],
        out_specs=pl.BlockSpec((None, hp * ATT_HEAD_DIM, tq), lambda b, h, i: (b, h, i)),
        out_shape=jax.ShapeDtypeStruct((B, H * ATT_HEAD_DIM, L - PREFIX), BF16),
        scratch_shapes=[pltpu.VMEM((hp, 1, tq), F32), pltpu.VMEM((hp, LANE, tq), F32),
                        pltpu.VMEM((hp, 1, tq), F32), pltpu.VMEM((hp, tq, tq), BF16)],
        compiler_params=pltpu.CompilerParams(
            dimension_semantics=("arbitrary", "arbitrary", "arbitrary"), vmem_limit_bytes=VMEM_LIMIT),
        name="attn",
    )(qt, k, vt)


def _ssm_kernel(u_ref, bre_ref, bim_ref, cre_ref, cim_ref, d_ref, negr_ref, negi_ref, posr_ref, posi_ref,
                nxtr_ref, nxti_ref, tri_ref, z_ref, cr_ref, ci_ref):
    tc = u_ref.shape[1]

    @pl.when(pl.program_id(1) == 0)
    def _():
        cr_ref[...] = jnp.zeros_like(cr_ref)
        ci_ref[...] = jnp.zeros_like(ci_ref)

    u = u_ref[...]
    ub = u.astype(BF16)
    n_state, n_chan = bre_ref.shape

    def block_diag_dot(w_ref, rhs, blocks):
        rb, cb = w_ref.shape[0] // blocks, w_ref.shape[1] // blocks
        return jnp.concatenate(
            [jnp.dot(w_ref[i * rb:(i + 1) * rb, i * cb:(i + 1) * cb], rhs[i * cb:(i + 1) * cb, :],
                     preferred_element_type=F32) for i in range(blocks)], axis=0)

    in_blocks = n_chan // MXU_TILE
    out_blocks = n_state // MXU_TILE
    bur = block_diag_dot(bre_ref, ub, in_blocks)
    bui = block_diag_dot(bim_ref, ub, in_blocks)
    nr, ni = negr_ref[...], negi_ref[...]
    tri = tri_ref[...]

    def prefix_sum(z):
        hi = z.astype(BF16)
        lo = (z - hi.astype(F32)).astype(BF16)
        return (jnp.dot(hi, tri, preferred_element_type=F32) + jnp.dot(lo, tri, preferred_element_type=F32))

    reps = tc // LANE
    tr = prefix_sum(nr * bur - ni * bui) + jnp.concatenate([cr_ref[...]] * reps, axis=1)
    ti = prefix_sum(nr * bui + ni * bur) + jnp.concatenate([ci_ref[...]] * reps, axis=1)
    pr, pi = posr_ref[...], posi_ref[...]
    xr = pr * tr - pi * ti
    xi = pr * ti + pi * tr
    lr, li = tr[:, tc - LANE:], ti[:, tc - LANE:]
    qr, qi = nxtr_ref[...], nxti_ref[...]
    cr_ref[...] = jnp.broadcast_to((qr * lr - qi * li)[:, LANE - 1:], cr_ref.shape)
    ci_ref[...] = jnp.broadcast_to((qr * li + qi * lr)[:, LANE - 1:], ci_ref.shape)
    y = (block_diag_dot(cre_ref, xr.astype(BF16), out_blocks)
         - block_diag_dot(cim_ref, xi.astype(BF16), out_blocks)
         + d_ref[...] * u)
    z_ref[...] = _gelu(y).T.astype(z_ref.dtype)


def _ssm(ut, bre, bim, cre, cim, d, tabs, tri):
    B, W, L = ut.shape
    tc = SSM_CHUNK
    first = (PREFIX - tc) // tc
    n_real = (L - PREFIX) // tc
    const = lambda shape: pl.BlockSpec(shape, lambda b, i: (0, 0), pipeline_mode=pl.Buffered(1))
    return pl.pallas_call(
        _ssm_kernel,
        grid=(B, 1 + n_real),
        in_specs=[pl.BlockSpec((None, W, tc), lambda b, i: (b, 0, i + first)),
                  const((SSM_LANES, W)), const((SSM_LANES, W)),
                  const((W, SSM_LANES)), const((W, SSM_LANES)), const((W, tc))]
                 + [const((SSM_LANES, tc))] * 4 + [const((SSM_LANES, LANE))] * 2 + [const((tc, tc))],
        out_specs=pl.BlockSpec((None, tc, W), lambda b, i: (b, jnp.maximum(i - 1, 0), 0)),
        out_shape=jax.ShapeDtypeStruct((B, L - PREFIX, W), BF16),
        scratch_shapes=[pltpu.VMEM((SSM_LANES, LANE), F32), pltpu.VMEM((SSM_LANES, LANE), F32)],
        compiler_params=pltpu.CompilerParams(
            dimension_semantics=("arbitrary", "arbitrary"), vmem_limit_bytes=VMEM_LIMIT),
        name="ssm",
    )(ut, bre, bim, cre, cim, d, *tabs, tri)


def _ssm_params(a_re, a_im, log_dt, b_re, b_im, c_re, c_im, d_skip):
    G, P = a_re.shape
    dt = jnp.exp(log_dt.astype(F32))[:, None]
    ar, ai = a_re.astype(F32), a_im.astype(F32)
    mag = jnp.exp(ar * dt)
    abar_r = mag * jnp.cos(ai * dt)
    abar_i = mag * jnp.sin(ai * dt)
    num_r, num_i = abar_r - 1.0, abar_i
    den = ar * ar + ai * ai
    coef_r = (num_r * ar + num_i * ai) / den
    coef_i = (num_i * ar - num_r * ai) / den
    br, bi = b_re.astype(F32), b_im.astype(F32)
    bbar_r = coef_r[..., None] * br - coef_i[..., None] * bi
    bbar_i = coef_r[..., None] * bi + coef_i[..., None] * br
    eye = jnp.eye(G, dtype=F32)
    bd_in = lambda w: jnp.einsum('gpc,gh->gphc', w, eye).reshape(G * P, G * SSM_GROUP)
    bd_out = lambda w: jnp.einsum('gcp,gh->gchp', w.astype(F32), eye).reshape(G * SSM_GROUP, G * P)
    k = jnp.arange(SSM_CHUNK, dtype=F32)[None, :]
    lr = (ar * dt).reshape(G * P, 1)
    li = (ai * dt).reshape(G * P, 1)

    def power(kk):
        m = jnp.exp(kk * lr)
        return m * jnp.cos(kk * li), m * jnp.sin(kk * li)

    mid = SSM_CHUNK // 2
    neg, pos, nxt = power(mid - k), power(k - mid), power(k[:, SSM_CHUNK - LANE:] + 1.0)
    d_tab = jnp.broadcast_to(d_skip.astype(F32).reshape(G * SSM_GROUP, 1), (G * SSM_GROUP, SSM_CHUNK))
    tri = jnp.triu(jnp.ones((SSM_CHUNK, SSM_CHUNK), F32)).astype(BF16)
    return (bd_in(bbar_r).astype(BF16), bd_in(bbar_i).astype(BF16),
            bd_out(c_re).astype(BF16), bd_out(c_im).astype(BF16), d_tab, (*neg, *pos, *nxt), tri)


def _merge_kernel(h_ref, att_ref, z_ref, g1_ref, wg_ref, bg_ref, wa_ref, wv_ref, wz_ref, wo_ref, g2_ref,
                  h1_ref, hn2_ref):
    D = h_ref.shape[1]
    h = h_ref[...]
    hn = _rms(h, g1_ref[...]).astype(BF16)
    gate = _sigmoid(jnp.dot(hn, wg_ref[...], preferred_element_type=F32) + bg_ref[...])
    branch_att = lax.dot_general(att_ref[...], wa_ref[...], (((0,), (0,)), ((), ())),
                                 preferred_element_type=F32)
    z = z_ref[...]
    branch_ssm = (jnp.dot(z, wv_ref[...], preferred_element_type=F32)
                  * _sigmoid(jnp.dot(z, wz_ref[...], preferred_element_type=F32)))
    merged = gate[:, :D] * branch_att + gate[:, D:] * branch_ssm
    h1 = h + jnp.dot(merged.astype(BF16), wo_ref[...], preferred_element_type=F32)
    h1_ref[...] = h1
    hn2_ref[...] = _rms(h1, g2_ref[...]).T.astype(hn2_ref.dtype)


def _merge(h, att, z, g1, wg, bg, wa, wv, wz, wo, g2, tm):
    B, S, D = h.shape
    const = lambda shape: pl.BlockSpec(shape, lambda b, i: (0,) * len(shape))
    tok = lambda w: pl.BlockSpec((None, tm, w), lambda b, i: (b, i, 0))
    return pl.pallas_call(
        _merge_kernel,
        grid=(B, S // tm),
        in_specs=[tok(D), pl.BlockSpec((None, ATT_WIDTH, tm), lambda b, i: (b, 0, i)),
                  tok(SSM_WIDTH), const((1, D)), const((D, 2 * D)), const((1, 2 * D)),
                  const((ATT_WIDTH, D)), const((SSM_WIDTH, D)), const((SSM_WIDTH, D)),
                  const((D, D)), const((1, D))],
        out_specs=[tok(D), pl.BlockSpec((None, D, tm), lambda b, i: (b, 0, i))],
        out_shape=[jax.ShapeDtypeStruct((B, S, D), F32), jax.ShapeDtypeStruct((B, D, S), BF16)],
        compiler_params=pltpu.CompilerParams(
            dimension_semantics=("arbitrary", "arbitrary"), vmem_limit_bytes=VMEM_LIMIT),
        name="merge",
    )(h, att, z, g1, wg, bg, wa, wv, wz, wo, g2)


def _top16(s, with_rank=False):
    vals = []
    rank = jnp.full(s.shape, float(PEER_TOPK), F32) if with_rank else None
    for a in range(PEER_TOPK):
        m = jnp.max(s, axis=0, keepdims=True)
        vals.append(m)
        hit = s == m
        if with_rank:
            rank = jnp.where(hit, float(a), rank)
        s = jnp.where(hit, -jnp.inf, s)
    return vals, rank


def _peer_kernel(hnt_ref, h1_ref, wq_ref, keys_ref, u_ref, vt_ref, go_ref, o_ref,
                 e1_ref, n1_ref, rank2_ref, e2_ref, acc_ref):
    c = pl.program_id(2)
    tq = hnt_ref.shape[1]
    first_keys = EXPERT_CHUNK // N_KEYS

    @pl.when(c == 0)
    def _():
        acc_ref[...] = jnp.zeros_like(acc_ref)
        hnt = hnt_ref[...]
        for h in range(PEER_HEADS):
            r = 2 * h * HALF_KEY
            qt = jnp.dot(wq_ref[r:r + 2 * HALF_KEY, :], hnt, preferred_element_type=F32).astype(BF16)
            s1 = jnp.dot(keys_ref[2 * h], qt[:HALF_KEY, :], preferred_element_type=F32)
            s2 = jnp.dot(keys_ref[2 * h + 1], qt[HALF_KEY:, :], preferred_element_type=F32)
            v1, _ = _top16(s1)
            v2, rank2 = _top16(s2, with_rank=True)
            v2_all = jnp.concatenate(v2, axis=0)
            v1_all = jnp.concatenate(v1, axis=0)
            cand = jnp.concatenate(
                [v1[0] + v2_all, v2[0] + v1_all, v2[1] + v1_all[:8]]
                + [v1[a] + v2_all[:8] for a in range(1, 5)], axis=0)
            best, _ = _top16(cand)
            tau = best[-1]
            zsum = sum(jnp.exp(b - best[0]) for b in best)
            n1 = jnp.zeros_like(s1)
            for a in range(PEER_TOPK):
                n_a = jnp.sum(jnp.where(v1[a] + v2_all >= tau, 1.0, 0.0), axis=0, keepdims=True)
                n1 = jnp.where(s1 == v1[a], n_a, n1)
            n1_ref[h] = n1
            rank2_ref[h] = rank2.astype(BF16)
            e1_ref[h] = jnp.exp(s1 - v1[0])
            e2_ref[h] = (jnp.exp(s2 - v2[0]) / zsum).astype(BF16)

    hnt = hnt_ref[...]
    ws = []
    for ii in range(first_keys):
        i = c * first_keys + ii
        act = jnp.dot(u_ref[ii * N_KEYS:(ii + 1) * N_KEYS, :], hnt, preferred_element_type=F32)
        g = jnp.zeros((N_KEYS, tq), BF16)
        for h in range(PEER_HEADS):
            n1 = n1_ref[h, pl.ds(i, 1), :].astype(BF16)
            e1 = e1_ref[h, pl.ds(i, 1), :].astype(BF16)
            g = g + jnp.where(rank2_ref[h] < n1, e2_ref[h] * e1, 0.0)
        ws.append(g * _gelu(act.astype(BF16)))
    acc_ref[...] += jnp.dot(vt_ref[...], jnp.concatenate(ws, axis=0), preferred_element_type=F32)

    @pl.when(c == pl.num_programs(2) - 1)
    def _():
        h2 = h1_ref[...] + acc_ref[...].T
        o_ref[...] = _rms(h2, go_ref[...])


def _peer(hn2t, h1, wq_t, keys, u, v_t, g_out, tq):
    B, S, D = h1.shape
    n_experts = u.shape[0]
    tok = pl.BlockSpec((None, tq, D), lambda b, i, c: (b, i, 0))
    const = lambda shape: pl.BlockSpec(shape, lambda b, i, c: (0,) * len(shape))
    per_head = lambda dtype: pltpu.VMEM((PEER_HEADS, N_KEYS, tq), dtype)
    return pl.pallas_call(
        _peer_kernel,
        grid=(B, S // tq, n_experts // EXPERT_CHUNK),
        in_specs=[pl.BlockSpec((None, D, tq), lambda b, i, c: (b, 0, i)),
                  tok, const(wq_t.shape), const(keys.shape),
                  pl.BlockSpec((EXPERT_CHUNK, D), lambda b, i, c: (c, 0)),
                  pl.BlockSpec((D, EXPERT_CHUNK), lambda b, i, c: (0, c)),
                  const((1, D))],
        out_specs=tok,
        out_shape=jax.ShapeDtypeStruct((B, S, D), F32),
        scratch_shapes=[per_head(F32), per_head(F32), per_head(BF16), per_head(BF16),
                        pltpu.VMEM((D, tq), F32)],
        compiler_params=pltpu.CompilerParams(
            dimension_semantics=("arbitrary", "arbitrary", "arbitrary"), vmem_limit_bytes=VMEM_LIMIT),
        name="peer",
    )(hn2t, h1, wq_t, keys, u, v_t, g_out)


def kernel(x, meta_tokens, mix_norm_g, w_in, b_forget, b_gate, w_att_branch, ssm_a_re, ssm_a_im, ssm_log_dt,
           ssm_b_re, ssm_b_im, ssm_c_re, ssm_c_im, ssm_d, w_glu_val, w_glu_gate, w_out, ffn_norm_g,
           w_query, sub_keys, expert_u, expert_v, out_norm_g):
    B, S, D = x.shape
    tm = TOKEN_TILE
    assert S % tm == 0 and w_in.shape[0] == 1, "one layer over a whole number of row tiles"
    layer = 0
    row = lambda v: v.astype(F32).reshape(1, -1)
    pre = jnp.concatenate([jnp.zeros((PREFIX - N_META, D), x.dtype), meta_tokens.astype(x.dtype)], axis=0)

    wl = w_in[layer]
    o_f = 3 * ATT_WIDTH
    o_u = o_f + ATT_HEADS
    o_g = o_u + SSM_WIDTH
    w1t = jnp.concatenate([wl[:, :ATT_WIDTH] * (ATT_HEAD_DIM ** -0.5), wl[:, 2 * ATT_WIDTH:o_f], wl[:, o_u:o_g]],
                          axis=1).T.astype(BF16)
    w1 = jnp.concatenate([wl[:, ATT_WIDTH:2 * ATT_WIDTH],
                          jnp.pad(wl[:, o_f:o_u], ((0, 0), (0, LANE - ATT_HEADS)))], axis=1).astype(BF16)
    bf = jnp.pad(b_forget[layer].astype(F32), (0, LANE - ATT_HEADS)).reshape(1, LANE)
    qt, k, vt, ut = _inproj(x, pre, row(mix_norm_g[layer]), w1t, w1, bf, tm)
    att = _attn(qt, k, vt, tm)
    z = _ssm(ut, *_ssm_params(
        ssm_a_re[layer], ssm_a_im[layer], ssm_log_dt[layer], ssm_b_re[layer], ssm_b_im[layer],
        ssm_c_re[layer], ssm_c_im[layer], ssm_d[layer]))
    h1, hn2t = _merge(x, att, z, row(mix_norm_g[layer]), wl[:, o_g:].astype(BF16), row(b_gate[layer]),
                      w_att_branch[layer].astype(BF16), w_glu_val[layer].astype(BF16),
                      w_glu_gate[layer].astype(BF16), w_out[layer].astype(BF16), row(ffn_norm_g[layer]), tm)
    keys = sub_keys[layer].astype(BF16).reshape(2 * PEER_HEADS, N_KEYS, HALF_KEY)
    return _peer(hn2t, h1, w_query[layer].T.astype(BF16), keys, expert_u[layer].astype(BF16),
                 expert_v[layer].T.astype(BF16), row(out_norm_g), tm)
```

```python
import jax
import jax.numpy as jnp
from jax import lax
from jax.experimental import pallas as pl
from jax.experimental.pallas import tpu as pltpu

F32 = jnp.float32
BF16 = jnp.bfloat16

N_META = 16
ATT_HEADS = 8
ATT_HEAD_DIM = 64
ATT_WIDTH = ATT_HEADS * ATT_HEAD_DIM
SSM_WIDTH = 512
SSM_GROUP = 16
SSM_GROUPS = SSM_WIDTH // SSM_GROUP
SSM_STATE = 64
SSM_LANES = SSM_GROUPS * SSM_STATE
PEER_HEADS = 8
N_KEYS = 128
PEER_TOPK = 16
HALF_KEY = 128
RMS_EPS = 1e-6
NEG_INF = -1e30

LANE = 128
MXU_TILE = 256
TOKEN_TILE = 512
PREFIX = TOKEN_TILE
META_TILE = 128
HEADS_PER_STEP = 2
KEY_TILES_PER_BODY = 2
SSM_CHUNK = 256
EXPERT_CHUNK = 2048
VMEM_LIMIT = 56 * 1024 * 1024


def _rms(x, g):
    return x * lax.rsqrt(jnp.mean(x * x, axis=-1, keepdims=True) + RMS_EPS) * g


def _gelu(x):
    return 0.5 * x * (1.0 + lax.erf(x * (2.0 ** -0.5)))


def _sigmoid(x):
    return 1.0 / (1.0 + jnp.exp(-x))


def _log_sigmoid(x):
    return jnp.minimum(x, 0.0) - jnp.log1p(jnp.exp(-jnp.abs(x)))


def _split3(c):
    hi = c.astype(BF16).astype(F32)
    r = c - hi
    mid = r.astype(BF16).astype(F32)
    lo = (r - mid).astype(BF16).astype(F32)
    return hi, mid, lo


def _inproj_kernel(x_ref, pre_ref, g_ref, wt_ref, w_ref, bf_ref, q_ref, k_ref, v_ref, u_ref, carry_ref):
    tm = x_ref.shape[0]
    i = pl.program_id(1)

    @pl.when(i == 0)
    def _():
        carry_ref[...] = jnp.zeros_like(carry_ref)

    h = jnp.where(i == 0, pre_ref[...], x_ref[...])
    hn = _rms(h, g_ref[...]).astype(BF16)
    proj_t = lax.dot_general(wt_ref[...], hn, (((1,), (1,)), ((), ())), preferred_element_type=F32)
    proj = jnp.dot(hn, w_ref[...], preferred_element_type=F32)
    u_ref[...] = proj_t[2 * ATT_WIDTH:, :]

    log_f = _log_sigmoid(proj[:, ATT_WIDTH:] + bf_ref[...])
    row = lax.broadcasted_iota(jnp.int32, (tm, tm), 0)
    col = lax.broadcasted_iota(jnp.int32, (tm, tm), 1)
    tri = jnp.where(row >= col, 1.0, 0.0).astype(BF16)
    c = carry_ref[...] + sum(jnp.dot(tri, piece.astype(BF16), preferred_element_type=F32)
                             for piece in _split3(log_f))
    carry_ref[...] = c[tm - 1:tm, :]
    hi, mid, lo = _split3(c)
    hi_t, mid_t, lo_t = _split3(c.T)

    lane = lax.broadcasted_iota(jnp.int32, (tm, ATT_HEAD_DIM), 1)
    sub = lax.broadcasted_iota(jnp.int32, (ATT_HEAD_DIM, tm), 0)
    ones3 = jnp.where(lane < 3, 1.0, 0.0)
    ones3_t = jnp.where((sub >= 3) & (sub < 6), 1.0, 0.0)
    v_ext_t = jnp.where(sub == 0, 1.0, 0.0)
    for hd in range(ATT_HEADS):
        lo_c, hi_c = hd * ATT_HEAD_DIM, (hd + 1) * ATT_HEAD_DIM
        q_ext_t = ones3_t + jnp.where(sub == 0, hi_t[hd:hd + 1], jnp.where(
            sub == 1, mid_t[hd:hd + 1], jnp.where(sub == 2, lo_t[hd:hd + 1], 0.0)))
        q_ref[hd] = jnp.concatenate([proj_t[lo_c:hi_c, :], q_ext_t], axis=0).astype(BF16)
        v_ref[hd] = jnp.concatenate([proj_t[ATT_WIDTH + lo_c:ATT_WIDTH + hi_c, :], v_ext_t], axis=0).astype(BF16)
        ch, cm, cl = hi[:, hd:hd + 1], mid[:, hd:hd + 1], lo[:, hd:hd + 1]
        k_ext = ones3 - jnp.where(lane == 3, ch, jnp.where(lane == 4, cm, jnp.where(lane == 5, cl, 0.0)))
        k_ref[hd] = jnp.concatenate([proj[:, lo_c:hi_c].astype(BF16), k_ext.astype(BF16)], axis=-1)


def _inproj(x, pre, g, wt, w, bf, tm):
    B, S, D = x.shape
    L = PREFIX + S
    ncol = w.shape[1]
    head_shape = jax.ShapeDtypeStruct((B, ATT_HEADS, L, LANE), BF16)
    head_spec = pl.BlockSpec((None, ATT_HEADS, tm, LANE), lambda b, i: (b, 0, i, 0))
    head_t_shape = jax.ShapeDtypeStruct((B, ATT_HEADS, LANE, L), BF16)
    head_t_spec = pl.BlockSpec((None, ATT_HEADS, LANE, tm), lambda b, i: (b, 0, 0, i))
    const = lambda shape: pl.BlockSpec(shape, lambda b, i: (0, 0))
    return pl.pallas_call(
        _inproj_kernel,
        grid=(B, L // tm),
        in_specs=[pl.BlockSpec((None, tm, D), lambda b, i: (b, jnp.maximum(i - 1, 0), 0)),
                  const((tm, D)), const((1, D)), const(wt.shape), const((D, ncol)), const((1, LANE))],
        out_specs=[head_t_spec, head_spec, head_t_spec,
                   pl.BlockSpec((None, SSM_WIDTH, tm), lambda b, i: (b, 0, i))],
        out_shape=[head_t_shape, head_shape, head_t_shape,
                   jax.ShapeDtypeStruct((B, SSM_WIDTH, L), F32)],
        scratch_shapes=[pltpu.VMEM((1, LANE), F32)],
        compiler_params=pltpu.CompilerParams(
            dimension_semantics=("arbitrary", "arbitrary"), vmem_limit_bytes=VMEM_LIMIT),
        name="inproj",
    )(x, pre, g, wt, w, bf)


def _attn_kernel(q_ref, k_ref, v_ref, o_ref, m_ref, acc_ref, alpha_ref, p_ref):
    qi = pl.program_id(2)
    tq = q_ref.shape[2]
    tk = tq
    heads = range(HEADS_PER_STEP)

    def scores(hh, start, size):
        return jnp.dot(k_ref[hh, pl.ds(start, size), :], q_ref[hh], preferred_element_type=F32)

    def tile_start(j):
        return pl.multiple_of(PREFIX + j * tk, tk)

    def softmax_tile(hh, s):
        m_prev = m_ref[hh]
        m_new = jnp.maximum(m_prev, jnp.max(s, axis=0, keepdims=True))
        m_ref[hh] = m_new
        alpha_ref[hh] = jnp.exp(m_prev - m_new)
        p_ref[hh] = jnp.exp(s - m_new).astype(BF16)

    def flush(hh, j):
        acc = alpha_ref[hh] * acc_ref[hh] + jnp.dot(v_ref[hh, :, pl.ds(tile_start(j), tk)], p_ref[hh],
                                                    preferred_element_type=F32)
        acc_ref[hh] = acc
        return acc

    key = lax.broadcasted_iota(jnp.int32, (META_TILE, tq), 0)
    for hh in heads:
        s = jnp.where(key >= META_TILE - N_META, scores(hh, PREFIX - META_TILE, META_TILE), NEG_INF)
        m0 = jnp.max(s, axis=0, keepdims=True)
        p0 = jnp.exp(s - m0).astype(BF16)
        acc_ref[hh] = jnp.dot(v_ref[hh, :, PREFIX - META_TILE:PREFIX], p0, preferred_element_type=F32)
        m_ref[hh] = m0

    key = lax.broadcasted_iota(jnp.int32, (tk, tq), 0)
    qry = lax.broadcasted_iota(jnp.int32, (tk, tq), 1)
    for hh in heads:
        softmax_tile(hh, jnp.where(key <= qry, scores(hh, tile_start(qi), tk), NEG_INF))

    def one_tile(j, pending):
        for hh in heads:
            flush(hh, pending)
            softmax_tile(hh, scores(hh, tile_start(j), tk))
        return j

    def unrolled(jj, pending):
        for t in range(KEY_TILES_PER_BODY):
            pending = one_tile(KEY_TILES_PER_BODY * jj + t, pending)
        return pending

    rest = qi % KEY_TILES_PER_BODY
    pending = lax.fori_loop(0, qi // KEY_TILES_PER_BODY, unrolled, qi)
    pending = lax.fori_loop(0, rest, lambda r, pend: one_tile(qi - rest + r, pend), pending)

    outs = []
    for hh in heads:
        acc = flush(hh, pending)
        outs.append(acc[:ATT_HEAD_DIM, :] / acc[ATT_HEAD_DIM:ATT_HEAD_DIM + 1, :])
    o_ref[...] = jnp.concatenate(outs, axis=0).astype(o_ref.dtype)


def _attn(qt, k, vt, tq):
    B, H, L, _ = k.shape
    first = PREFIX // tq
    hp = HEADS_PER_STEP
    return pl.pallas_call(
        _attn_kernel,
        grid=(B, H // hp, (L - PREFIX) // tq),
        in_specs=[pl.BlockSpec((None, hp, LANE, tq), lambda b, h, i: (b, h, 0, i + first)),
                  pl.BlockSpec((None, hp, L, LANE), lambda b, h, i: (b, h, 0, 0)),
                  pl.BlockSpec((None, hp, LANE, L), lambda b, h, i: (b, h, 0, 0))],
        out_specs=pl.BlockSpec((None, hp * ATT_HEAD_DIM, tq), lambda b, h, i: (b, h, i)),
        out_shape=jax.ShapeDtypeStruct((B, H * ATT_HEAD_DIM, L - PREFIX), BF16),
        scratch_shapes=[pltpu.VMEM((hp, 1, tq), F32), pltpu.VMEM((hp, LANE, tq), F32),
                        pltpu.VMEM((hp, 1, tq), F32), pltpu.VMEM((hp, tq, tq), BF16)],
        compiler_params=pltpu.CompilerParams(
            dimension_semantics=("arbitrary", "arbitrary", "arbitrary"), vmem_limit_bytes=VMEM_LIMIT),
        name="attn",
    )(qt, k, vt)


def _ssm_kernel(u_ref, bre_ref, bim_ref, cre_ref, cim_ref, d_ref, negr_ref, negi_ref, posr_ref, posi_ref,
                nxtr_ref, nxti_ref, tri_ref, z_ref, cr_ref, ci_ref):
    tc = u_ref.shape[1]

    @pl.when(pl.program_id(1) == 0)
    def _():
        cr_ref[...] = jnp.zeros_like(cr_ref)
        ci_ref[...] = jnp.zeros_like(ci_ref)

    u = u_ref[...]
    ub = u.astype(BF16)
    n_state, n_chan = bre_ref.shape

    def block_diag_dot(w_ref, rhs, blocks):
        rb, cb = w_ref.shape[0] // blocks, w_ref.shape[1] // blocks
        return jnp.concatenate(
            [jnp.dot(w_ref[i * rb:(i + 1) * rb, i * cb:(i + 1) * cb], rhs[i * cb:(i + 1) * cb, :],
                     preferred_element_type=F32) for i in range(blocks)], axis=0)

    in_blocks = n_chan // MXU_TILE
    out_blocks = n_state // MXU_TILE
    bur = block_diag_dot(bre_ref, ub, in_blocks)
    bui = block_diag_dot(bim_ref, ub, in_blocks)
    nr, ni = negr_ref[...], negi_ref[...]
    tri = tri_ref[...]

    def prefix_sum(z):
        hi = z.astype(BF16)
        lo = (z - hi.astype(F32)).astype(BF16)
        return (jnp.dot(hi, tri, preferred_element_type=F32) + jnp.dot(lo, tri, preferred_element_type=F32))

    reps = tc // LANE
    tr = prefix_sum(nr * bur - ni * bui) + jnp.concatenate([cr_ref[...]] * reps, axis=1)
    ti = prefix_sum(nr * bui + ni * bur) + jnp.concatenate([ci_ref[...]] * reps, axis=1)
    pr, pi = posr_ref[...], posi_ref[...]
    xr = pr * tr - pi * ti
    xi = pr * ti + pi * tr
    lr, li = tr[:, tc - LANE:], ti[:, tc - LANE:]
    qr, qi = nxtr_ref[...], nxti_ref[...]
    cr_ref[...] = jnp.broadcast_to((qr * lr - qi * li)[:, LANE - 1:], cr_ref.shape)
    ci_ref[...] = jnp.broadcast_to((qr * li + qi * lr)[:, LANE - 1:], ci_ref.shape)
    y = (block_diag_dot(cre_ref, xr.astype(BF16), out_blocks)
         - block_diag_dot(cim_ref, xi.astype(BF16), out_blocks)
         + d_ref[...] * u)
    z_ref[...] = _gelu(y).T.astype(z_ref.dtype)


def _ssm(ut, bre, bim, cre, cim, d, tabs, tri):
    B, W, L = ut.shape
    tc = SSM_CHUNK
    first = (PREFIX - tc) // tc
    n_real = (L - PREFIX) // tc
    const = lambda shape: pl.BlockSpec(shape, lambda b, i: (0, 0), pipeline_mode=pl.Buffered(1))
    return pl.pallas_call(
        _ssm_kernel,
        grid=(B, 1 + n_real),
        in_specs=[pl.BlockSpec((None, W, tc), lambda b, i: (b, 0, i + first)),
                  const((SSM_LANES, W)), const((SSM_LANES, W)),
                  const((W, SSM_LANES)), const((W, SSM_LANES)), const((W, tc))]
                 + [const((SSM_LANES, tc))] * 4 + [const((SSM_LANES, LANE))] * 2 + [const((tc, tc))],
        out_specs=pl.BlockSpec((None, tc, W), lambda b, i: (b, jnp.maximum(i - 1, 0), 0)),
        out_shape=jax.ShapeDtypeStruct((B, L - PREFIX, W), BF16),
        scratch_shapes=[pltpu.VMEM((SSM_LANES, LANE), F32), pltpu.VMEM((SSM_LANES, LANE), F32)],
        compiler_params=pltpu.CompilerParams(
            dimension_semantics=("arbitrary", "arbitrary"), vmem_limit_bytes=VMEM_LIMIT),
        name="ssm",
    )(ut, bre, bim, cre, cim, d, *tabs, tri)


def _ssm_params(a_re, a_im, log_dt, b_re, b_im, c_re, c_im, d_skip):
    G, P = a_re.shape
    dt = jnp.exp(log_dt.astype(F32))[:, None]
    ar, ai = a_re.astype(F32), a_im.astype(F32)
    mag = jnp.exp(ar * dt)
    abar_r = mag * jnp.cos(ai * dt)
    abar_i = mag * jnp.sin(ai * dt)
    num_r, num_i = abar_r - 1.0, abar_i
    den = ar * ar + ai * ai
    coef_r = (num_r * ar + num_i * ai) / den
    coef_i = (num_i * ar - num_r * ai) / den
    br, bi = b_re.astype(F32), b_im.astype(F32)
    bbar_r = coef_r[..., None] * br - coef_i[..., None] * bi
    bbar_i = coef_r[..., None] * bi + coef_i[..., None] * br
    eye = jnp.eye(G, dtype=F32)
    bd_in = lambda w: jnp.einsum('gpc,gh->gphc', w, eye).reshape(G * P, G * SSM_GROUP)
    bd_out = lambda w: jnp.einsum('gcp,gh->gchp', w.astype(F32), eye).reshape(G * SSM_GROUP, G * P)
    k = jnp.arange(SSM_CHUNK, dtype=F32)[None, :]
    lr = (ar * dt).reshape(G * P, 1)
    li = (ai * dt).reshape(G * P, 1)

    def power(kk):
        m = jnp.exp(kk * lr)
        return m * jnp.cos(kk * li), m * jnp.sin(kk * li)

    mid = SSM_CHUNK // 2
    neg, pos, nxt = power(mid - k), power(k - mid), power(k[:, SSM_CHUNK - LANE:] + 1.0)
    d_tab = jnp.broadcast_to(d_skip.astype(F32).reshape(G * SSM_GROUP, 1), (G * SSM_GROUP, SSM_CHUNK))
    tri = jnp.triu(jnp.ones((SSM_CHUNK, SSM_CHUNK), F32)).astype(BF16)
    return (bd_in(bbar_r).astype(BF16), bd_in(bbar_i).astype(BF16),
            bd_out(c_re).astype(BF16), bd_out(c_im).astype(BF16), d_tab, (*neg, *pos, *nxt), tri)


def _merge_kernel(h_ref, att_ref, z_ref, g1_ref, wg_ref, bg_ref, wa_ref, wv_ref, wz_ref, wo_ref, g2_ref,
                  h1_ref, hn2_ref):
    D = h_ref.shape[1]
    h = h_ref[...]
    hn = _rms(h, g1_ref[...]).astype(BF16)
    gate = _sigmoid(jnp.dot(hn, wg_ref[...], preferred_element_type=F32) + bg_ref[...])
    branch_att = lax.dot_general(att_ref[...], wa_ref[...], (((0,), (0,)), ((), ())),
                                 preferred_element_type=F32)
    z = z_ref[...]
    branch_ssm = (jnp.dot(z, wv_ref[...], preferred_element_type=F32)
                  * _sigmoid(jnp.dot(z, wz_ref[...], preferred_element_type=F32)))
    merged = gate[:, :D] * branch_att + gate[:, D:] * branch_ssm
    h1 = h + jnp.dot(merged.astype(BF16), wo_ref[...], preferred_element_type=F32)
    h1_ref[...] = h1
    hn2_ref[...] = _rms(h1, g2_ref[...]).T.astype(hn2_ref.dtype)


def _merge(h, att, z, g1, wg, bg, wa, wv, wz, wo, g2, tm):
    B, S, D = h.shape
    const = lambda shape: pl.BlockSpec(shape, lambda b, i: (0,) * len(shape))
    tok = lambda w: pl.BlockSpec((None, tm, w), lambda b, i: (b, i, 0))
    return pl.pallas_call(
        _merge_kernel,
        grid=(B, S // tm),
        in_specs=[tok(D), pl.BlockSpec((None, ATT_WIDTH, tm), lambda b, i: (b, 0, i)),
                  tok(SSM_WIDTH), const((1, D)), const((D, 2 * D)), const((1, 2 * D)),
                  const((ATT_WIDTH, D)), const((SSM_WIDTH, D)), const((SSM_WIDTH, D)),
                  const((D, D)), const((1, D))],
        out_specs=[tok(D), pl.BlockSpec((None, D, tm), lambda b, i: (b, 0, i))],
        out_shape=[jax.ShapeDtypeStruct((B, S, D), F32), jax.ShapeDtypeStruct((B, D, S), BF16)],
        compiler_params=pltpu.CompilerParams(
            dimension_semantics=("arbitrary", "arbitrary"), vmem_limit_bytes=VMEM_LIMIT),
        name="merge",
    )(h, att, z, g1, wg, bg, wa, wv, wz, wo, g2)


def _top16(s, with_rank=False):
    vals = []
    rank = jnp.full(s.shape, float(PEER_TOPK), F32) if with_rank else None
    for a in range(PEER_TOPK):
        m = jnp.max(s, axis=0, keepdims=True)
        vals.append(m)
        hit = s == m
        if with_rank:
            rank = jnp.where(hit, float(a), rank)
        s = jnp.where(hit, -jnp.inf, s)
    return vals, rank


def _peer_kernel(hnt_ref, h1_ref, wq_ref, keys_ref, u_ref, vt_ref, go_ref, o_ref,
                 e1_ref, n1_ref, rank2_ref, e2_ref, acc_ref):
    c = pl.program_id(2)
    tq = hnt_ref.shape[1]
    first_keys = EXPERT_CHUNK // N_KEYS

    @pl.when(c == 0)
    def _():
        acc_ref[...] = jnp.zeros_like(acc_ref)
        qt = jnp.dot(wq_ref[...], hnt_ref[...], preferred_element_type=F32).astype(BF16)
        for h in range(PEER_HEADS):
            r = 2 * h * HALF_KEY
            s1 = jnp.dot(keys_ref[2 * h], qt[r:r + HALF_KEY, :], preferred_element_type=F32)
            s2 = jnp.dot(keys_ref[2 * h + 1], qt[r + HALF_KEY:r + 2 * HALF_KEY, :],
                         preferred_element_type=F32)
            v1, _ = _top16(s1)
            v2, rank2 = _top16(s2, with_rank=True)
            v2_all = jnp.concatenate(v2, axis=0)
            v1_all = jnp.concatenate(v1, axis=0)
            cand = jnp.concatenate(
                [v1[0] + v2_all, v2[0] + v1_all, v2[1] + v1_all[:8]]
                + [v1[a] + v2_all[:8] for a in range(1, 5)], axis=0)
            best, _ = _top16(cand)
            tau = best[-1]
            zsum = sum(jnp.exp(b - best[0]) for b in best)
            n1 = jnp.zeros_like(s1)
            for a in range(PEER_TOPK):
                n_a = jnp.sum(jnp.where(v1[a] + v2_all >= tau, 1.0, 0.0), axis=0, keepdims=True)
                n1 = jnp.where(s1 == v1[a], n_a, n1)
            n1_ref[h] = n1
            rank2_ref[h] = rank2.astype(BF16)
            e1_ref[h] = jnp.exp(s1 - v1[0])
            e2_ref[h] = (jnp.exp(s2 - v2[0]) / zsum).astype(BF16)

    hnt = hnt_ref[...]
    ws = []
    for ii in range(first_keys):
        i = c * first_keys + ii
        act = jnp.dot(u_ref[ii * N_KEYS:(ii + 1) * N_KEYS, :], hnt, preferred_element_type=F32)
        g = jnp.zeros((N_KEYS, tq), BF16)
        for h in range(PEER_HEADS):
            n1 = n1_ref[h, pl.ds(i, 1), :].astype(BF16)
            e1 = e1_ref[h, pl.ds(i, 1), :].astype(BF16)
            g = g + jnp.where(rank2_ref[h] < n1, e2_ref[h] * e1, 0.0)
        ws.append(g * _gelu(act.astype(BF16)))
    acc_ref[...] += jnp.dot(vt_ref[...], jnp.concatenate(ws, axis=0), preferred_element_type=F32)

    @pl.when(c == pl.num_programs(2) - 1)
    def _():
        h2 = h1_ref[...] + acc_ref[...].T
        o_ref[...] = _rms(h2, go_ref[...])


def _peer(hn2t, h1, wq_t, keys, u, v_t, g_out, tq):
    B, S, D = h1.shape
    n_experts = u.shape[0]
    tok = pl.BlockSpec((None, tq, D), lambda b, i, c: (b, i, 0))
    const = lambda shape: pl.BlockSpec(shape, lambda b, i, c: (0,) * len(shape))
    per_head = lambda dtype: pltpu.VMEM((PEER_HEADS, N_KEYS, tq), dtype)
    return pl.pallas_call(
        _peer_kernel,
        grid=(B, S // tq, n_experts // EXPERT_CHUNK),
        in_specs=[pl.BlockSpec((None, D, tq), lambda b, i, c: (b, 0, i)),
                  tok, const(wq_t.shape), const(keys.shape),
                  pl.BlockSpec((EXPERT_CHUNK, D), lambda b, i, c: (c, 0)),
                  pl.BlockSpec((D, EXPERT_CHUNK), lambda b, i, c: (0, c)),
                  const((1, D))],
        out_specs=tok,
        out_shape=jax.ShapeDtypeStruct((B, S, D), F32),
        scratch_shapes=[per_head(F32), per_head(F32), per_head(BF16), per_head(BF16),
                        pltpu.VMEM((D, tq), F32)],
        compiler_params=pltpu.CompilerParams(
            dimension_semantics=("arbitrary", "arbitrary", "arbitrary"), vmem_limit_bytes=VMEM_LIMIT),
        name="peer",
    )(hn2t, h1, wq_t, keys, u, v_t, g_out)


def kernel(x, meta_tokens, mix_norm_g, w_in, b_forget, b_gate, w_att_branch, ssm_a_re, ssm_a_im, ssm_log_dt,
           ssm_b_re, ssm_b_im, ssm_c_re, ssm_c_im, ssm_d, w_glu_val, w_glu_gate, w_out, ffn_norm_g,
           w_query, sub_keys, expert_u, expert_v, out_norm_g):
    B, S, D = x.shape
    tm = TOKEN_TILE
    assert S % tm == 0 and w_in.shape[0] == 1, "one layer over a whole number of row tiles"
    layer = 0
    row = lambda v: v.astype(F32).reshape(1, -1)
    pre = jnp.concatenate([jnp.zeros((PREFIX - N_META, D), x.dtype), meta_tokens.astype(x.dtype)], axis=0)

    wl = w_in[layer]
    o_f = 3 * ATT_WIDTH
    o_u = o_f + ATT_HEADS
    o_g = o_u + SSM_WIDTH
    w1t = jnp.concatenate([wl[:, :ATT_WIDTH] * (ATT_HEAD_DIM ** -0.5), wl[:, 2 * ATT_WIDTH:o_f], wl[:, o_u:o_g]],
                          axis=1).T.astype(BF16)
    w1 = jnp.concatenate([wl[:, ATT_WIDTH:2 * ATT_WIDTH],
                          jnp.pad(wl[:, o_f:o_u], ((0, 0), (0, LANE - ATT_HEADS)))], axis=1).astype(BF16)
    bf = jnp.pad(b_forget[layer].astype(F32), (0, LANE - ATT_HEADS)).reshape(1, LANE)
    qt, k, vt, ut = _inproj(x, pre, row(mix_norm_g[layer]), w1t, w1, bf, tm)
    att = _attn(qt, k, vt, tm)
    z = _ssm(ut, *_ssm_params(
        ssm_a_re[layer], ssm_a_im[layer], ssm_log_dt[layer], ssm_b_re[layer], ssm_b_im[layer],
        ssm_c_re[layer], ssm_c_im[layer], ssm_d[layer]))
    h1, hn2t = _merge(x, att, z, row(mix_norm_g[layer]), wl[:, o_g:].astype(BF16), row(b_gate[layer]),
                      w_att_branch[layer].astype(BF16), w_glu_val[layer].astype(BF16),
                      w_glu_gate[layer].astype(BF16), w_out[layer].astype(BF16), row(ffn_norm_g[layer]), tm)
    keys = sub_keys[layer].astype(BF16).reshape(2 * PEER_HEADS, N_KEYS, HALF_KEY)
    return _peer(hn2t, h1, w_query[layer].T.astype(BF16), keys, expert_u[layer].astype(BF16),
                 expert_v[layer].T.astype(BF16), row(out_norm_g), tm)
```

```python
import jax
import jax.numpy as jnp
from jax import lax
from jax.experimental import pallas as pl
from jax.experimental.pallas import tpu as pltpu

F32 = jnp.float32
BF16 = jnp.bfloat16

N_META = 16
ATT_HEADS = 8
ATT_HEAD_DIM = 64
ATT_WIDTH = ATT_HEADS * ATT_HEAD_DIM
SSM_WIDTH = 512
SSM_GROUP = 16
SSM_GROUPS = SSM_WIDTH // SSM_GROUP
SSM_STATE = 64
SSM_LANES = SSM_GROUPS * SSM_STATE
PEER_HEADS = 8
N_KEYS = 128
PEER_TOPK = 16
HALF_KEY = 128
RMS_EPS = 1e-6
NEG_INF = -1e30

LANE = 128
MXU_TILE = 256
TOKEN_TILE = 512
PREFIX = TOKEN_TILE
META_TILE = 128
HEADS_PER_STEP = 2
KEY_TILES_PER_BODY = 4
SSM_CHUNK = 256
EXPERT_CHUNK = 2048
VMEM_LIMIT = 56 * 1024 * 1024


def _rms(x, g):
    return x * lax.rsqrt(jnp.mean(x * x, axis=-1, keepdims=True) + RMS_EPS) * g


def _gelu(x):
    return 0.5 * x * (1.0 + lax.erf(x * (2.0 ** -0.5)))


def _sigmoid(x):
    return 1.0 / (1.0 + jnp.exp(-x))


def _log_sigmoid(x):
    return jnp.minimum(x, 0.0) - jnp.log1p(jnp.exp(-jnp.abs(x)))


def _split3(c):
    hi = c.astype(BF16).astype(F32)
    r = c - hi
    mid = r.astype(BF16).astype(F32)
    lo = (r - mid).astype(BF16).astype(F32)
    return hi, mid, lo


def _inproj_kernel(x_ref, pre_ref, g_ref, wt_ref, w_ref, bf_ref, q_ref, k_ref, v_ref, u_ref, carry_ref):
    tm = x_ref.shape[0]
    i = pl.program_id(1)

    @pl.when(i == 0)
    def _():
        carry_ref[...] = jnp.zeros_like(carry_ref)

    h = jnp.where(i == 0, pre_ref[...], x_ref[...])
    hn = _rms(h, g_ref[...]).astype(BF16)
    proj_t = lax.dot_general(wt_ref[...], hn, (((1,), (1,)), ((), ())), preferred_element_type=F32)
    proj = jnp.dot(hn, w_ref[...], preferred_element_type=F32)
    u_ref[...] = proj_t[2 * ATT_WIDTH:, :]

    log_f = _log_sigmoid(proj[:, ATT_WIDTH:] + bf_ref[...])
    row = lax.broadcasted_iota(jnp.int32, (tm, tm), 0)
    col = lax.broadcasted_iota(jnp.int32, (tm, tm), 1)
    tri = jnp.where(row >= col, 1.0, 0.0).astype(BF16)
    c = carry_ref[...] + sum(jnp.dot(tri, piece.astype(BF16), preferred_element_type=F32)
                             for piece in _split3(log_f))
    carry_ref[...] = c[tm - 1:tm, :]
    hi, mid, lo = _split3(c)
    hi_t, mid_t, lo_t = _split3(c.T)

    lane = lax.broadcasted_iota(jnp.int32, (tm, ATT_HEAD_DIM), 1)
    sub = lax.broadcasted_iota(jnp.int32, (ATT_HEAD_DIM, tm), 0)
    ones3 = jnp.where(lane < 3, 1.0, 0.0)
    ones3_t = jnp.where((sub >= 3) & (sub < 6), 1.0, 0.0)
    v_ext_t = jnp.where(sub == 0, 1.0, 0.0)
    for hd in range(ATT_HEADS):
        lo_c, hi_c = hd * ATT_HEAD_DIM, (hd + 1) * ATT_HEAD_DIM
        q_ext_t = ones3_t + jnp.where(sub == 0, hi_t[hd:hd + 1], jnp.where(
            sub == 1, mid_t[hd:hd + 1], jnp.where(sub == 2, lo_t[hd:hd + 1], 0.0)))
        q_ref[hd] = jnp.concatenate([proj_t[lo_c:hi_c, :], q_ext_t], axis=0).astype(BF16)
        v_ref[hd] = jnp.concatenate([proj_t[ATT_WIDTH + lo_c:ATT_WIDTH + hi_c, :], v_ext_t], axis=0).astype(BF16)
        ch, cm, cl = hi[:, hd:hd + 1], mid[:, hd:hd + 1], lo[:, hd:hd + 1]
        k_ext = ones3 - jnp.where(lane == 3, ch, jnp.where(lane == 4, cm, jnp.where(lane == 5, cl, 0.0)))
        k_ref[hd] = jnp.concatenate([proj[:, lo_c:hi_c].astype(BF16), k_ext.astype(BF16)], axis=-1)


def _inproj(x, pre, g, wt, w, bf, tm):
    B, S, D = x.shape
    L = PREFIX + S
    ncol = w.shape[1]
    head_shape = jax.ShapeDtypeStruct((B, ATT_HEADS, L, LANE), BF16)
    head_spec = pl.BlockSpec((None, ATT_HEADS, tm, LANE), lambda b, i: (b, 0, i, 0))
    head_t_shape = jax.ShapeDtypeStruct((B, ATT_HEADS, LANE, L), BF16)
    head_t_spec = pl.BlockSpec((None, ATT_HEADS, LANE, tm), lambda b, i: (b, 0, 0, i))
    const = lambda shape: pl.BlockSpec(shape, lambda b, i: (0, 0))
    return pl.pallas_call(
        _inproj_kernel,
        grid=(B, L // tm),
        in_specs=[pl.BlockSpec((None, tm, D), lambda b, i: (b, jnp.maximum(i - 1, 0), 0)),
                  const((tm, D)), const((1, D)), const(wt.shape), const((D, ncol)), const((1, LANE))],
        out_specs=[head_t_spec, head_spec, head_t_spec,
                   pl.BlockSpec((None, SSM_WIDTH, tm), lambda b, i: (b, 0, i))],
        out_shape=[head_t_shape, head_shape, head_t_shape,
                   jax.ShapeDtypeStruct((B, SSM_WIDTH, L), F32)],
        scratch_shapes=[pltpu.VMEM((1, LANE), F32)],
        compiler_params=pltpu.CompilerParams(
            dimension_semantics=("arbitrary", "arbitrary"), vmem_limit_bytes=VMEM_LIMIT),
        name="inproj",
    )(x, pre, g, wt, w, bf)


def _attn_kernel(q_ref, k_ref, v_ref, o_ref, m_ref, acc_ref, alpha_ref, p_ref):
    qi = pl.program_id(2)
    tq = q_ref.shape[2]
    tk = tq
    heads = range(HEADS_PER_STEP)

    def scores(hh, start, size):
        return jnp.dot(k_ref[hh, pl.ds(start, size), :], q_ref[hh], preferred_element_type=F32)

    def tile_start(j):
        return pl.multiple_of(PREFIX + j * tk, tk)

    def softmax_tile(hh, s):
        m_prev = m_ref[hh]
        m_new = jnp.maximum(m_prev, jnp.max(s, axis=0, keepdims=True))
        m_ref[hh] = m_new
        alpha_ref[hh] = jnp.exp(m_prev - m_new)
        p_ref[hh] = jnp.exp(s - m_new).astype(BF16)

    def flush(hh, j):
        acc = alpha_ref[hh] * acc_ref[hh] + jnp.dot(v_ref[hh, :, pl.ds(tile_start(j), tk)], p_ref[hh],
                                                    preferred_element_type=F32)
        acc_ref[hh] = acc
        return acc

    key = lax.broadcasted_iota(jnp.int32, (META_TILE, tq), 0)
    for hh in heads:
        s = jnp.where(key >= META_TILE - N_META, scores(hh, PREFIX - META_TILE, META_TILE), NEG_INF)
        m0 = jnp.max(s, axis=0, keepdims=True)
        p0 = jnp.exp(s - m0).astype(BF16)
        acc_ref[hh] = jnp.dot(v_ref[hh, :, PREFIX - META_TILE:PREFIX], p0, preferred_element_type=F32)
        m_ref[hh] = m0

    key = lax.broadcasted_iota(jnp.int32, (tk, tq), 0)
    qry = lax.broadcasted_iota(jnp.int32, (tk, tq), 1)
    for hh in heads:
        softmax_tile(hh, jnp.where(key <= qry, scores(hh, tile_start(qi), tk), NEG_INF))

    def one_tile(j, pending):
        for hh in heads:
            flush(hh, pending)
            softmax_tile(hh, scores(hh, tile_start(j), tk))
        return j

    def unrolled(jj, pending):
        for t in range(KEY_TILES_PER_BODY):
            pending = one_tile(KEY_TILES_PER_BODY * jj + t, pending)
        return pending

    rest = qi % KEY_TILES_PER_BODY
    pending = lax.fori_loop(0, qi // KEY_TILES_PER_BODY, unrolled, qi)
    pending = lax.fori_loop(0, rest, lambda r, pend: one_tile(qi - rest + r, pend), pending)

    outs = []
    for hh in heads:
        acc = flush(hh, pending)
        outs.append(acc[:ATT_HEAD_DIM, :] / acc[ATT_HEAD_DIM:ATT_HEAD_DIM + 1, :])
    o_ref[...] = jnp.concatenate(outs, axis=0).astype(o_ref.dtype)


def _attn(qt, k, vt, tq):
    B, H, L, _ = k.shape
    first = PREFIX // tq
    hp = HEADS_PER_STEP
    return pl.pallas_call(
        _attn_kernel,
        grid=(B, H // hp, (L - PREFIX) // tq),
        in_specs=[pl.BlockSpec((None, hp, LANE, tq), lambda b, h, i: (b, h, 0, i + first)),
                  pl.BlockSpec((None, hp, L, LANE), lambda b, h, i: (b, h, 0, 0)),
                  pl.BlockSpec((None, hp, LANE, L), lambda b, h, i: (b, h, 0, 0))],
        out_specs=pl.BlockSpec((None, hp * ATT_HEAD_DIM, tq), lambda b, h, i: (b, h, i)),
        out_shape=jax.ShapeDtypeStruct((B, H * ATT_HEAD_DIM, L - PREFIX), BF16),
        scratch_shapes=[pltpu.VMEM((hp, 1, tq), F32), pltpu.VMEM((hp, LANE, tq), F32),
                        pltpu.VMEM((hp, 1, tq), F32), pltpu.VMEM((hp, tq, tq), BF16)],
        compiler_params=pltpu.CompilerParams(
            dimension_semantics=("arbitrary", "arbitrary", "arbitrary"), vmem_limit_bytes=VMEM_LIMIT),
        name="attn",
    )(qt, k, vt)


def _ssm_kernel(u_ref, bre_ref, bim_ref, cre_ref, cim_ref, d_ref, negr_ref, negi_ref, posr_ref, posi_ref,
                nxtr_ref, nxti_ref, tri_ref, z_ref, cr_ref, ci_ref):
    tc = u_ref.shape[1]

    @pl.when(pl.program_id(1) == 0)
    def _():
        cr_ref[...] = jnp.zeros_like(cr_ref)
        ci_ref[...] = jnp.zeros_like(ci_ref)

    u = u_ref[...]
    ub = u.astype(BF16)
    n_state, n_chan = bre_ref.shape

    def block_diag_dot(w_ref, rhs, blocks):
        rb, cb = w_ref.shape[0] // blocks, w_ref.shape[1] // blocks
        return jnp.concatenate(
            [jnp.dot(w_ref[i * rb:(i + 1) * rb, i * cb:(i + 1) * cb], rhs[i * cb:(i + 1) * cb, :],
                     preferred_element_type=F32) for i in range(blocks)], axis=0)

    in_blocks = n_chan // MXU_TILE
    out_blocks = n_state // MXU_TILE
    bur = block_diag_dot(bre_ref, ub, in_blocks)
    bui = block_diag_dot(bim_ref, ub, in_blocks)
    nr, ni = negr_ref[...], negi_ref[...]
    tri = tri_ref[...]

    def prefix_sum(z):
        hi = z.astype(BF16)
        lo = (z - hi.astype(F32)).astype(BF16)
        return (jnp.dot(hi, tri, preferred_element_type=F32) + jnp.dot(lo, tri, preferred_element_type=F32))

    reps = tc // LANE
    tr = prefix_sum(nr * bur - ni * bui) + jnp.concatenate([cr_ref[...]] * reps, axis=1)
    ti = prefix_sum(nr * bui + ni * bur) + jnp.concatenate([ci_ref[...]] * reps, axis=1)
    pr, pi = posr_ref[...], posi_ref[...]
    xr = pr * tr - pi * ti
    xi = pr * ti + pi * tr
    lr, li = tr[:, tc - LANE:], ti[:, tc - LANE:]
    qr, qi = nxtr_ref[...], nxti_ref[...]
    cr_ref[...] = jnp.broadcast_to((qr * lr - qi * li)[:, LANE - 1:], cr_ref.shape)
    ci_ref[...] = jnp.broadcast_to((qr * li + qi * lr)[:, LANE - 1:], ci_ref.shape)
    y = (block_diag_dot(cre_ref, xr.astype(BF16), out_blocks)
         - block_diag_dot(cim_ref, xi.astype(BF16), out_blocks)
         + d_ref[...] * u)
    z_ref[...] = _gelu(y).T.astype(z_ref.dtype)


def _ssm(ut, bre, bim, cre, cim, d, tabs, tri):
    B, W, L = ut.shape
    tc = SSM_CHUNK
    first = (PREFIX - tc) // tc
    n_real = (L - PREFIX) // tc
    const = lambda shape: pl.BlockSpec(shape, lambda b, i: (0, 0), pipeline_mode=pl.Buffered(1))
    return pl.pallas_call(
        _ssm_kernel,
        grid=(B, 1 + n_real),
        in_specs=[pl.BlockSpec((None, W, tc), lambda b, i: (b, 0, i + first)),
                  const((SSM_LANES, W)), const((SSM_LANES, W)),
                  const((W, SSM_LANES)), const((W, SSM_LANES)), const((W, tc))]
                 + [const((SSM_LANES, tc))] * 4 + [const((SSM_LANES, LANE))] * 2 + [const((tc, tc))],
        out_specs=pl.BlockSpec((None, tc, W), lambda b, i: (b, jnp.maximum(i - 1, 0), 0)),
        out_shape=jax.ShapeDtypeStruct((B, L - PREFIX, W), BF16),
        scratch_shapes=[pltpu.VMEM((SSM_LANES, LANE), F32), pltpu.VMEM((SSM_LANES, LANE), F32)],
        compiler_params=pltpu.CompilerParams(
            dimension_semantics=("arbitrary", "arbitrary"), vmem_limit_bytes=VMEM_LIMIT),
        name="ssm",
    )(ut, bre, bim, cre, cim, d, *tabs, tri)


def _ssm_params(a_re, a_im, log_dt, b_re, b_im, c_re, c_im, d_skip):
    G, P = a_re.shape
    dt = jnp.exp(log_dt.astype(F32))[:, None]
    ar, ai = a_re.astype(F32), a_im.astype(F32)
    mag = jnp.exp(ar * dt)
    abar_r = mag * jnp.cos(ai * dt)
    abar_i = mag * jnp.sin(ai * dt)
    num_r, num_i = abar_r - 1.0, abar_i
    den = ar * ar + ai * ai
    coef_r = (num_r * ar + num_i * ai) / den
    coef_i = (num_i * ar - num_r * ai) / den
    br, bi = b_re.astype(F32), b_im.astype(F32)
    bbar_r = coef_r[..., None] * br - coef_i[..., None] * bi
    bbar_i = coef_r[..., None] * bi + coef_i[..., None] * br
    eye = jnp.eye(G, dtype=F32)
    bd_in = lambda w: jnp.einsum('gpc,gh->gphc', w, eye).reshape(G * P, G * SSM_GROUP)
    bd_out = lambda w: jnp.einsum('gcp,gh->gchp', w.astype(F32), eye).reshape(G * SSM_GROUP, G * P)
    k = jnp.arange(SSM_CHUNK, dtype=F32)[None, :]
    lr = (ar * dt).reshape(G * P, 1)
    li = (ai * dt).reshape(G * P, 1)

    def power(kk):
        m = jnp.exp(kk * lr)
        return m * jnp.cos(kk * li), m * jnp.sin(kk * li)

    mid = SSM_CHUNK // 2
    neg, pos, nxt = power(mid - k), power(k - mid), power(k[:, SSM_CHUNK - LANE:] + 1.0)
    d_tab = jnp.broadcast_to(d_skip.astype(F32).reshape(G * SSM_GROUP, 1), (G * SSM_GROUP, SSM_CHUNK))
    tri = jnp.triu(jnp.ones((SSM_CHUNK, SSM_CHUNK), F32)).astype(BF16)
    return (bd_in(bbar_r).astype(BF16), bd_in(bbar_i).astype(BF16),
            bd_out(c_re).astype(BF16), bd_out(c_im).astype(BF16), d_tab, (*neg, *pos, *nxt), tri)


def _merge_kernel(h_ref, att_ref, z_ref, g1_ref, wg_ref, bg_ref, wa_ref, wv_ref, wz_ref, wo_ref, g2_ref,
                  h1_ref, hn2_ref):
    D = h_ref.shape[1]
    h = h_ref[...]
    hn = _rms(h, g1_ref[...]).astype(BF16)
    gate = _sigmoid(jnp.dot(hn, wg_ref[...], preferred_element_type=F32) + bg_ref[...])
    branch_att = lax.dot_general(att_ref[...], wa_ref[...], (((0,), (0,)), ((), ())),
                                 preferred_element_type=F32)
    z = z_ref[...]
    branch_ssm = (jnp.dot(z, wv_ref[...], preferred_element_type=F32)
                  * _sigmoid(jnp.dot(z, wz_ref[...], preferred_element_type=F32)))
    merged = gate[:, :D] * branch_att + gate[:, D:] * branch_ssm
    h1 = h + jnp.dot(merged.astype(BF16), wo_ref[...], preferred_element_type=F32)
    h1_ref[...] = h1
    hn2_ref[...] = _rms(h1, g2_ref[...]).T.astype(hn2_ref.dtype)


def _merge(h, att, z, g1, wg, bg, wa, wv, wz, wo, g2, tm):
    B, S, D = h.shape
    const = lambda shape: pl.BlockSpec(shape, lambda b, i: (0,) * len(shape))
    tok = lambda w: pl.BlockSpec((None, tm, w), lambda b, i: (b, i, 0))
    return pl.pallas_call(
        _merge_kernel,
        grid=(B, S // tm),
        in_specs=[tok(D), pl.BlockSpec((None, ATT_WIDTH, tm), lambda b, i: (b, 0, i)),
                  tok(SSM_WIDTH), const((1, D)), const((D, 2 * D)), const((1, 2 * D)),
                  const((ATT_WIDTH, D)), const((SSM_WIDTH, D)), const((SSM_WIDTH, D)),
                  const((D, D)), const((1, D))],
        out_specs=[tok(D), pl.BlockSpec((None, D, tm), lambda b, i: (b, 0, i))],
        out_shape=[jax.ShapeDtypeStruct((B, S, D), F32), jax.ShapeDtypeStruct((B, D, S), BF16)],
        compiler_params=pltpu.CompilerParams(
            dimension_semantics=("arbitrary", "arbitrary"), vmem_limit_bytes=VMEM_LIMIT),
        name="merge",
    )(h, att, z, g1, wg, bg, wa, wv, wz, wo, g2)


def _top16(s, with_rank=False):
    vals = []
    rank = jnp.full(s.shape, float(PEER_TOPK), F32) if with_rank else None
    for a in range(PEER_TOPK):
        m = jnp.max(s, axis=0, keepdims=True)
        vals.append(m)
        hit = s == m
        if with_rank:
            rank = jnp.where(hit, float(a), rank)
        s = jnp.where(hit, -jnp.inf, s)
    return vals, rank


def _peer_kernel(hnt_ref, h1_ref, wq_ref, keys_ref, u_ref, vt_ref, go_ref, o_ref,
                 e1_ref, n1_ref, rank2_ref, e2_ref, acc_ref):
    c = pl.program_id(2)
    tq = hnt_ref.shape[1]
    first_keys = EXPERT_CHUNK // N_KEYS

    @pl.when(c == 0)
    def _():
        acc_ref[...] = jnp.zeros_like(acc_ref)
        qt = jnp.dot(wq_ref[...], hnt_ref[...], preferred_element_type=F32).astype(BF16)
        for h in range(PEER_HEADS):
            r = 2 * h * HALF_KEY
            s1 = jnp.dot(keys_ref[2 * h], qt[r:r + HALF_KEY, :], preferred_element_type=F32)
            s2 = jnp.dot(keys_ref[2 * h + 1], qt[r + HALF_KEY:r + 2 * HALF_KEY, :],
                         preferred_element_type=F32)
            v1, _ = _top16(s1)
            v2, rank2 = _top16(s2, with_rank=True)
            v2_all = jnp.concatenate(v2, axis=0)
            v1_all = jnp.concatenate(v1, axis=0)
            cand = jnp.concatenate(
                [v1[0] + v2_all, v2[0] + v1_all, v2[1] + v1_all[:8]]
                + [v1[a] + v2_all[:8] for a in range(1, 5)], axis=0)
            best, _ = _top16(cand)
            tau = best[-1]
            zsum = sum(jnp.exp(b - best[0]) for b in best)
            n1 = jnp.zeros_like(s1)
            for a in range(PEER_TOPK):
                n_a = jnp.sum(jnp.where(v1[a] + v2_all >= tau, 1.0, 0.0), axis=0, keepdims=True)
                n1 = jnp.where(s1 == v1[a], n_a, n1)
            n1_ref[h] = n1
            rank2_ref[h] = rank2.astype(BF16)
            e1_ref[h] = jnp.exp(s1 - v1[0])
            e2_ref[h] = (jnp.exp(s2 - v2[0]) / zsum).astype(BF16)

    hnt = hnt_ref[...]
    ws = []
    for ii in range(first_keys):
        i = c * first_keys + ii
        act = jnp.dot(u_ref[ii * N_KEYS:(ii + 1) * N_KEYS, :], hnt, preferred_element_type=F32)
        g = jnp.zeros((N_KEYS, tq), BF16)
        for h in range(PEER_HEADS):
            n1 = n1_ref[h, pl.ds(i, 1), :].astype(BF16)
            e1 = e1_ref[h, pl.ds(i, 1), :].astype(BF16)
            g = g + jnp.where(rank2_ref[h] < n1, e2_ref[h] * e1, 0.0)
        ws.append(g * _gelu(act.astype(BF16)))
    acc_ref[...] += jnp.dot(vt_ref[...], jnp.concatenate(ws, axis=0), preferred_element_type=F32)

    @pl.when(c == pl.num_programs(2) - 1)
    def _():
        h2 = h1_ref[...] + acc_ref[...].T
        o_ref[...] = _rms(h2, go_ref[...])


def _peer(hn2t, h1, wq_t, keys, u, v_t, g_out, tq):
    B, S, D = h1.shape
    n_experts = u.shape[0]
    tok = pl.BlockSpec((None, tq, D), lambda b, i, c: (b, i, 0))
    const = lambda shape: pl.BlockSpec(shape, lambda b, i, c: (0,) * len(shape))
    per_head = lambda dtype: pltpu.VMEM((PEER_HEADS, N_KEYS, tq), dtype)
    return pl.pallas_call(
        _peer_kernel,
        grid=(B, S // tq, n_experts // EXPERT_CHUNK),
        in_specs=[pl.BlockSpec((None, D, tq), lambda b, i, c: (b, 0, i)),
                  tok, const(wq_t.shape), const(keys.shape),
                  pl.BlockSpec((EXPERT_CHUNK, D), lambda b, i, c: (c, 0)),
                  pl.BlockSpec((D, EXPERT_CHUNK), lambda b, i, c: (0, c)),
                  const((1, D))],
        out_specs=tok,
        out_shape=jax.ShapeDtypeStruct((B, S, D), F32),
        scratch_shapes=[per_head(F32), per_head(F32), per_head(BF16), per_head(BF16),
                        pltpu.VMEM((D, tq), F32)],
        compiler_params=pltpu.CompilerParams(
            dimension_semantics=("arbitrary", "arbitrary", "arbitrary"), vmem_limit_bytes=VMEM_LIMIT),
        name="peer",
    )(hn2t, h1, wq_t, keys, u, v_t, g_out)


def kernel(x, meta_tokens, mix_norm_g, w_in, b_forget, b_gate, w_att_branch, ssm_a_re, ssm_a_im, ssm_log_dt,
           ssm_b_re, ssm_b_im, ssm_c_re, ssm_c_im, ssm_d, w_glu_val, w_glu_gate, w_out, ffn_norm_g,
           w_query, sub_keys, expert_u, expert_v, out_norm_g):
    B, S, D = x.shape
    tm = TOKEN_TILE
    assert S % tm == 0 and w_in.shape[0] == 1, "one layer over a whole number of row tiles"
    layer = 0
    row = lambda v: v.astype(F32).reshape(1, -1)
    pre = jnp.concatenate([jnp.zeros((PREFIX - N_META, D), x.dtype), meta_tokens.astype(x.dtype)], axis=0)

    wl = w_in[layer]
    o_f = 3 * ATT_WIDTH
    o_u = o_f + ATT_HEADS
    o_g = o_u + SSM_WIDTH
    w1t = jnp.concatenate([wl[:, :ATT_WIDTH] * (ATT_HEAD_DIM ** -0.5), wl[:, 2 * ATT_WIDTH:o_f], wl[:, o_u:o_g]],
                          axis=1).T.astype(BF16)
    w1 = jnp.concatenate([wl[:, ATT_WIDTH:2 * ATT_WIDTH],
                          jnp.pad(wl[:, o_f:o_u], ((0, 0), (0, LANE - ATT_HEADS)))], axis=1).astype(BF16)
    bf = jnp.pad(b_forget[layer].astype(F32), (0, LANE - ATT_HEADS)).reshape(1, LANE)
    qt, k, vt, ut = _inproj(x, pre, row(mix_norm_g[layer]), w1t, w1, bf, tm)
    att = _attn(qt, k, vt, tm)
    z = _ssm(ut, *_ssm_params(
        ssm_a_re[layer], ssm_a_im[layer], ssm_log_dt[layer], ssm_b_re[layer], ssm_b_im[layer],
        ssm_c_re[layer], ssm_c_im[layer], ssm_d[layer]))
    h1, hn2t = _merge(x, att, z, row(mix_norm_g[layer]), wl[:, o_g:].astype(BF16), row(b_gate[layer]),
                      w_att_branch[layer].astype(BF16), w_glu_val[layer].astype(BF16),
                      w_glu_gate[layer].astype(BF16), w_out[layer].astype(BF16), row(ffn_norm_g[layer]), tm)
    keys = sub_keys[layer].astype(BF16).reshape(2 * PEER_HEADS, N_KEYS, HALF_KEY)
    return _peer(hn2t, h1, w_query[layer].T.astype(BF16), keys, expert_u[layer].astype(BF16),
                 expert_v[layer].T.astype(BF16), row(out_norm_g), tm)
```
